```python
import math
import jax, jax.numpy as jnp
from jax import lax
import numpy as np

D_MODEL = 1024
BATCH = 8
SEQ = 2048
DEPTH = 4

HEAD_DIM = 64
A_GROUPS = ((128, 1), (512, 4), (2048, 16))
A_HEADS_PER_GROUP = 4
A_HEADS = A_HEADS_PER_GROUP * len(A_GROUPS)
A_WIDTH = A_HEADS * HEAD_DIM
B_HEADS = D_MODEL // HEAD_DIM
B_WIDTH = B_HEADS * HEAD_DIM
N_A = DEPTH // 2
N_B = DEPTH - N_A
D_FF = 2816
CONV_W = 3
ROPE_DIM = HEAD_DIM // 4
ROPE_THETA = 500000.0
BLK = 128
EPS = 1e-6
NEG = -1e30

kernel_name = "yoco_dilated_fox_convffn_trunk"


def rms_norm(x, g):
    x32 = x.astype(jnp.float32)
    y = x32 * lax.rsqrt(jnp.mean(x32 * x32, axis=-1, keepdims=True) + EPS)
    return (y * g.astype(jnp.float32)).astype(x.dtype)


def rope_tables(T):
    pos = jnp.arange(T, dtype=jnp.float32)
    inv = ROPE_THETA ** (-jnp.arange(0, ROPE_DIM, 2, dtype=jnp.float32) / ROPE_DIM)
    ang = pos[:, None] * inv[None, :]
    return jnp.cos(ang), jnp.sin(ang)


def apply_rope(t, cos, sin):
    half = ROPE_DIM // 2
    c = cos[None, :, None, :].astype(t.dtype)
    s = sin[None, :, None, :].astype(t.dtype)
    x1 = t[..., :half]
    x2 = t[..., half:ROPE_DIM]
    return jnp.concatenate([x1 * c - x2 * s, x2 * c + x1 * s, t[..., ROPE_DIM:]], axis=-1)


def banded_attention(q, k, v, n_back):
    N, L, H, D = q.shape
    nb = L // BLK
    qb = q.reshape(N, nb, BLK, H, D)
    kb = k.reshape(N, nb, BLK, H, D)
    vb = v.reshape(N, nb, BLK, H, D)

    def with_prev(t):
        prev = jnp.pad(t[:, :-1], ((0, 0), (1, 0), (0, 0), (0, 0), (0, 0)))
        return jnp.concatenate([prev, t], axis=2)

    kc, vc = with_prev(kb), with_prev(vb)
    s = jnp.einsum('nbqhd,nbkhd->nbhqk', qb, kc).astype(jnp.float32) * (D ** -0.5)
    rel = (jnp.arange(BLK)[:, None] + BLK) - jnp.arange(2 * BLK)[None, :]
    band = (rel >= 0) & (rel <= n_back)
    has_prev = (jnp.arange(nb)[:, None, None] > 0) | (jnp.arange(2 * BLK)[None, None, :] >= BLK)
    mask = band[None] & has_prev
    s = jnp.where(mask[None, :, None], s, NEG)
    m = jnp.max(s, axis=-1, keepdims=True)
    p = jnp.exp(s - m)
    l = jnp.sum(p, axis=-1, keepdims=True)
    o = jnp.einsum('nbhqk,nbkhd->nbqhd', (p / l).astype(v.dtype), vc)
    lse = (m + jnp.log(l))[..., 0]
    return o.reshape(N, L, H, D), lse.transpose(0, 1, 3, 2).reshape(N, L, H)


def dilated_mixer(xn, w_qkv, w_o, cos, sin):
    B, T, _ = xn.shape
    qkv = xn @ w_qkv
    q, k, v = jnp.split(qkv, 3, axis=-1)
    q = apply_rope(q.reshape(B, T, A_HEADS, HEAD_DIM), cos, sin)
    k = apply_rope(k.reshape(B, T, A_HEADS, HEAD_DIM), cos, sin)
    v = v.reshape(B, T, A_HEADS, HEAD_DIM)
    G = A_HEADS_PER_GROUP
    outs, lses = [], []
    for g, (window, r) in enumerate(A_GROUPS):
        L = T // r
        Lp = -(-L // BLK) * BLK

        def gather(t):
            t = t[:, :, g * G:(g + 1) * G].reshape(B, L, r, G, HEAD_DIM)
            t = t.transpose(0, 2, 1, 3, 4).reshape(B * r, L, G, HEAD_DIM)
            return jnp.pad(t, ((0, 0), (0, Lp - L), (0, 0), (0, 0)))

        o, lse = banded_attention(gather(q), gather(k), gather(v), window // r)
        o = o[:, :L].reshape(B, r, L, G, HEAD_DIM).transpose(0, 2, 1, 3, 4).reshape(B, T, G, HEAD_DIM)
        lse = lse[:, :L].reshape(B, r, L, G).transpose(0, 2, 1, 3).reshape(B, T, G)
        outs.append(o)
        lses.append(lse)
    alpha = jax.nn.softmax(jnp.stack(lses, axis=0), axis=0)
    o = jnp.concatenate([outs[g] * alpha[g][..., None].astype(outs[g].dtype)
                         for g in range(len(A_GROUPS))], axis=2)
    return o.reshape(B, T, A_WIDTH) @ w_o


def fox_mixer(xn, w_q, w_o, k, v, c):
    B, T, _ = xn.shape
    q = (xn @ w_q).reshape(B, T, B_HEADS, HEAD_DIM)
    c_t = c.transpose(0, 2, 1)
    scale = HEAD_DIM ** -0.5
    outs = []
    for i in range(T // BLK):
        q0, q1 = i * BLK, (i + 1) * BLK
        s = jnp.einsum('bqhd,bkhd->bhqk', q[:, q0:q1], k[:, :q1]).astype(jnp.float32) * scale
        s = s + (c_t[:, :, q0:q1, None] - c_t[:, :, None, :q1])
        causal = jnp.arange(q0, q1)[:, None] >= jnp.arange(q1)[None, :]
        s = jnp.where(causal, s, NEG)
        p = jax.nn.softmax(s, axis=-1).astype(v.dtype)
        outs.append(jnp.einsum('bhqk,bkhd->bqhd', p, v[:, :q1]))
    o = jnp.concatenate(outs, axis=1)
    return o.reshape(B, T, B_WIDTH) @ w_o


def conv_ffn(xn, w_up, cw, cb, w_down):
    a = xn @ w_up
    T = a.shape[1]
    ap = jnp.pad(a, ((0, 0), (CONV_W - 1, 0), (0, 0)))
    a = sum(ap[:, j:j + T] * cw[j] for j in range(CONV_W)) + cb
    gate, val = jnp.split(a, 2, axis=-1)
    return (jax.nn.gelu(gate, approximate=True) * val) @ w_down


def setup_inputs(seed: int = 0) -> dict:
    key = jax.random.key(seed)
    ks = jax.random.split(key, 14)

    def nrm(k, shape, fan_in):
        return jax.random.normal(k, shape, jnp.float32) * fan_in ** -0.5

    return {
        "x": jax.random.normal(ks[0], (BATCH, SEQ, D_MODEL), jnp.float32),
        "norm_gains": 1.0 + 0.05 * jax.random.normal(ks[1], (DEPTH, 4, D_MODEL), jnp.float32),
        "w_qkv_a": nrm(ks[2], (N_A, D_MODEL, 3 * A_WIDTH), D_MODEL),
        "w_o_a": nrm(ks[3], (N_A, A_WIDTH, D_MODEL), A_WIDTH),
        "w_q_b": nrm(ks[4], (N_B, D_MODEL, B_WIDTH), D_MODEL),
        "w_o_b": nrm(ks[5], (N_B, B_WIDTH, D_MODEL), B_WIDTH),
        "kv_norm": 1.0 + 0.05 * jax.random.normal(ks[6], (D_MODEL,), jnp.float32),
        "w_kvf": nrm(ks[7], (D_MODEL, 2 * B_WIDTH + B_HEADS), D_MODEL),
        "b_f": 3.0 + 0.5 * jax.random.normal(ks[8], (B_HEADS,), jnp.float32),
        "w_up": nrm(ks[9], (DEPTH, D_MODEL, 2 * D_FF), D_MODEL),
        "conv_w": nrm(ks[10], (DEPTH, CONV_W, 2 * D_FF), CONV_W),
        "conv_b": 0.01 * jax.random.normal(ks[11], (DEPTH, 2 * D_FF), jnp.float32),
        "w_down": nrm(ks[12], (DEPTH, D_FF, D_MODEL), D_FF),
    }


def reference(x, norm_gains, w_qkv_a, w_o_a, w_q_b, w_o_b, kv_norm, w_kvf, b_f,
              w_up, conv_w, conv_b, w_down):
    B, T, _ = x.shape
    cos, sin = rope_tables(T)
    h = x
    k_sh = v_sh = c_sh = None
    for l in range(DEPTH):
        g = norm_gains[l]
        if l < N_A:
            mix = dilated_mixer(rms_norm(h, g[0]), w_qkv_a[l], w_o_a[l], cos, sin)
        else:
            if l == N_A:
                kvf = rms_norm(h, kv_norm) @ w_kvf
                k_sh = kvf[..., :B_WIDTH].reshape(B, T, B_HEADS, HEAD_DIM)
                v_sh = kvf[..., B_WIDTH:2 * B_WIDTH].reshape(B, T, B_HEADS, HEAD_DIM)
                log_f = jax.nn.log_sigmoid((kvf[..., 2 * B_WIDTH:] + b_f).astype(jnp.float32))
                c_sh = jnp.cumsum(log_f, axis=1)
            j = l - N_A
            mix = fox_mixer(rms_norm(h, g[0]), w_q_b[j], w_o_b[j], k_sh, v_sh, c_sh)
        h = h + rms_norm(mix, g[1])
        f = conv_ffn(rms_norm(h, g[2]), w_up[l], conv_w[l], conv_b[l], w_down[l])
        h = h + rms_norm(f, g[3])
    return h
```

```python
import functools

import jax
import jax.numpy as jnp
from jax import lax
from jax.experimental import pallas as pl
from jax.experimental.pallas import tpu as pltpu

D_MODEL = 1024
DEPTH = 4
HEAD_DIM = 64
A_GROUPS = ((128, 1), (512, 4), (2048, 16))
A_HEADS_PER_GROUP = 4
A_HEADS = A_HEADS_PER_GROUP * len(A_GROUPS)
A_WIDTH = A_HEADS * HEAD_DIM
GROUP_WIDTH = A_HEADS_PER_GROUP * HEAD_DIM
B_HEADS = D_MODEL // HEAD_DIM
B_WIDTH = B_HEADS * HEAD_DIM
N_A = DEPTH // 2
D_FF = 2816
CONV_W = 3
ROPE_DIM = HEAD_DIM // 4
ROPE_HALF = ROPE_DIM // 2
ROPE_THETA = 500000.0
BLK = 128
EPS = 1e-6
NEG = -1e30
SCALE = HEAD_DIM ** -0.5

LANES = 128
SUBLANES = 8
MXU_DIM = 256
VMEM_LIMIT = 56 * 1024 * 1024

ROW_TILE = 512
FFN_CHUNK = MXU_DIM
FOX_TILE = 256

F32 = jnp.float32
BF16 = jnp.bfloat16


def _params(n_axes):
    return pltpu.CompilerParams(dimension_semantics=("arbitrary",) * n_axes,
                                vmem_limit_bytes=VMEM_LIMIT)


def _resident(shape, index_map):
    return pl.BlockSpec(shape, index_map, pipeline_mode=pl.Buffered(1))


def _rms(x, g):
    return x * lax.rsqrt(jnp.mean(x * x, axis=-1, keepdims=True) + EPS) * g


def _qkv_kernel(x_ref, g_ref, w_ref, cos_ref, sa_ref, sb_ref, o_ref):
    xn = _rms(x_ref[...], g_ref[...]).astype(BF16)
    y = jnp.dot(xn, w_ref[...], preferred_element_type=F32)
    cos, sa, sb = cos_ref[...], sa_ref[...], sb_ref[...]
    n_rope = 2 * A_WIDTH // LANES
    for c in range(n_rope):
        t = y[:, c * LANES:(c + 1) * LANES]
        r = t * cos + pltpu.roll(t, ROPE_HALF, 1) * sa + pltpu.roll(t, LANES - ROPE_HALF, 1) * sb
        if c < A_WIDTH // LANES:
            r = r * SCALE
        o_ref[:, c * LANES:(c + 1) * LANES] = r.astype(BF16)
    o_ref[:, 2 * A_WIDTH:] = y[:, 2 * A_WIDTH:].astype(BF16)


def _qkv_proj(h2, g, w, cos_t, sa_t, sb_t, seq):
    m = h2.shape[0]
    n = w.shape[1]
    tm = ROW_TILE
    tpb = seq // tm
    tab = pl.BlockSpec((tm, LANES), lambda i: (i % tpb, 0))
    return pl.pallas_call(
        _qkv_kernel,
        grid=(m // tm,),
        in_specs=[pl.BlockSpec((tm, D_MODEL), lambda i: (i, 0)),
                  _resident((1, D_MODEL), lambda i: (0, 0)),
                  _resident((D_MODEL, n), lambda i: (0, 0)),
                  tab, tab, tab],
        out_specs=pl.BlockSpec((tm, n), lambda i: (i, 0)),
        out_shape=jax.ShapeDtypeStruct((m, n), BF16),
        compiler_params=_params(1),
        name="qkv_proj",
    )(h2, g, w, cos_t, sa_t, sb_t)


def _dilated_kernel(q_ref, k_ref, v_ref, o_ref, lse_ref, *, nb, n_back):
    lane = lax.broadcasted_iota(jnp.int32, (BLK, GROUP_WIDTH), 1)
    head_masks = [(lane >= h * HEAD_DIM) & (lane < (h + 1) * HEAD_DIM)
                  for h in range(A_HEADS_PER_GROUP)]
    qi = lax.broadcasted_iota(jnp.int32, (BLK, 2 * BLK), 0)
    kj = lax.broadcasted_iota(jnp.int32, (BLK, 2 * BLK), 1)
    rel = qi + BLK - kj
    band2 = (rel >= 0) & (rel <= n_back)
    qi1 = lax.broadcasted_iota(jnp.int32, (BLK, BLK), 0)
    kj1 = lax.broadcasted_iota(jnp.int32, (BLK, BLK), 1)
    band1 = (qi1 - kj1 >= 0) & (qi1 - kj1 <= n_back)

    def block(q, kk, vv, band):
        o_all = jnp.zeros((BLK, GROUP_WIDTH), F32)
        lse_all = jnp.zeros((BLK, GROUP_WIDTH), F32)
        for h in range(A_HEADS_PER_GROUP):
            qm = jnp.where(head_masks[h], q, jnp.zeros_like(q))
            s = lax.dot_general(qm, kk, (((1,), (1,)), ((), ())), preferred_element_type=F32)
            s = jnp.where(band, s, NEG)
            m = jnp.max(s, axis=-1, keepdims=True)
            p = jnp.exp(s - m)
            l = jnp.sum(p, axis=-1, keepdims=True)
            pv = jnp.dot((p / l).astype(BF16), vv, preferred_element_type=F32)
            o_all = jnp.where(head_masks[h], pv, o_all)
            lse_all = jnp.where(head_masks[h], m + jnp.log(l), lse_all)
        return o_all, lse_all

    o0, l0 = block(q_ref[0:BLK, :], k_ref[0:BLK, :], v_ref[0:BLK, :], band1)
    o_ref[0:BLK, :] = o0.astype(BF16)
    lse_ref[0:BLK, :] = l0

    def body(i, carry):
        r0 = pl.multiple_of(i * BLK, BLK)
        p0 = pl.multiple_of((i - 1) * BLK, BLK)
        o, l = block(q_ref[pl.ds(r0, BLK), :], k_ref[pl.ds(p0, 2 * BLK), :],
                     v_ref[pl.ds(p0, 2 * BLK), :], band2)
        o_ref[pl.ds(r0, BLK), :] = o.astype(BF16)
        lse_ref[pl.ds(r0, BLK), :] = l
        return carry

    if nb > 1:
        lax.fori_loop(1, nb, body, 0)


def _dilated_attention(qkv, g, window, r, batch, seq):
    ln = seq // r
    assert ln % BLK == 0
    nblk = 3 * A_WIDTH // GROUP_WIDTH
    ngrp = len(A_GROUPS)
    qkv_v = qkv.reshape(batch, ln, r * 3 * A_WIDTH)

    def in_spec(which):
        return pl.BlockSpec((None, ln, GROUP_WIDTH),
                            lambda b, j: (b, 0, j * nblk + which * ngrp + g))

    out_spec = pl.BlockSpec((None, ln, GROUP_WIDTH), lambda b, j: (b, 0, j))
    o, lse = pl.pallas_call(
        functools.partial(_dilated_kernel, nb=ln // BLK, n_back=window // r),
        grid=(batch, r),
        in_specs=[in_spec(0), in_spec(1), in_spec(2)],
        out_specs=[out_spec, out_spec],
        out_shape=[jax.ShapeDtypeStruct((batch, ln, r * GROUP_WIDTH), BF16),
                   jax.ShapeDtypeStruct((batch, ln, r * GROUP_WIDTH), F32)],
        compiler_params=_params(2),
        name=f"dilated_attn_g{g}",
    )(qkv_v, qkv_v, qkv_v)
    return (o.reshape(batch * seq, GROUP_WIDTH), lse.reshape(batch * seq, GROUP_WIDTH))


def _oproj_a_kernel(o0_ref, o1_ref, o2_ref, l0_ref, l1_ref, l2_ref, w_ref, h_ref, g_ref, out_ref):
    l0, l1, l2 = l0_ref[...], l1_ref[...], l2_ref[...]
    mx = jnp.maximum(jnp.maximum(l0, l1), l2)
    e0, e1, e2 = jnp.exp(l0 - mx), jnp.exp(l1 - mx), jnp.exp(l2 - mx)
    den = e0 + e1 + e2
    parts = [(o_ref[...].astype(F32) * (e / den)).astype(BF16)
             for o_ref, e in ((o0_ref, e0), (o1_ref, e1), (o2_ref, e2))]
    mix = jnp.dot(jnp.concatenate(parts, axis=1), w_ref[...], preferred_element_type=F32)
    out_ref[...] = h_ref[...] + _rms(mix, g_ref[...])


def _oproj_a(os_, ls_, w, h2, g):
    m = h2.shape[0]
    tm = ROW_TILE
    grp = pl.BlockSpec((tm, GROUP_WIDTH), lambda i: (i, 0))
    row = pl.BlockSpec((tm, D_MODEL), lambda i: (i, 0))
    return pl.pallas_call(
        _oproj_a_kernel,
        grid=(m // tm,),
        in_specs=[grp] * 6 + [_resident((A_WIDTH, D_MODEL), lambda i: (0, 0)), row,
                              _resident((1, D_MODEL), lambda i: (0, 0))],
        out_specs=row,
        out_shape=jax.ShapeDtypeStruct((m, D_MODEL), F32),
        compiler_params=_params(1),
        name="oproj_a",
    )(*os_, *ls_, w, h2, g)


def _oproj_b_kernel(o_ref, w_ref, h_ref, g_ref, out_ref):
    mix = jnp.dot(o_ref[...], w_ref[...], preferred_element_type=F32)
    out_ref[...] = h_ref[...] + _rms(mix, g_ref[...])


def _oproj_b(o, w, h2, g):
    m = h2.shape[0]
    tm = ROW_TILE
    row = pl.BlockSpec((tm, D_MODEL), lambda i: (i, 0))
    return pl.pallas_call(
        _oproj_b_kernel,
        grid=(m // tm,),
        in_specs=[pl.BlockSpec((tm, B_WIDTH), lambda i: (i, 0)),
                  _resident((B_WIDTH, D_MODEL), lambda i: (0, 0)), row,
                  _resident((1, D_MODEL), lambda i: (0, 0))],
        out_specs=row,
        out_shape=jax.ShapeDtypeStruct((m, D_MODEL), F32),
        compiler_params=_params(1),
        name="oproj_b",
    )(o, w, h2, g)


def _ffn_kernel(h_ref, g2_ref, wup_ref, cw_ref, cb_ref, wdn_ref, g3_ref, out_ref,
                carry_ref, abuf_ref, hmid_ref, *, tm):
    @pl.when(pl.program_id(1) == 0)
    def _():
        carry_ref[...] = jnp.zeros_like(carry_ref)

    h = h_ref[...]
    xn = _rms(h, g2_ref[...]).astype(BF16)

    def conv(col, slot):
        cs = slice(col, col + FFN_CHUNK)
        a = jnp.dot(xn, wup_ref[:, cs], preferred_element_type=F32)
        abuf_ref[slot, 0:SUBLANES, :] = carry_ref[:, cs]
        abuf_ref[slot, SUBLANES:SUBLANES + tm, :] = a
        carry_ref[:, cs] = a[tm - SUBLANES:tm, :]
        a1 = abuf_ref[slot, SUBLANES - 1:SUBLANES - 1 + tm, :]
        a2 = abuf_ref[slot, SUBLANES - 2:SUBLANES - 2 + tm, :]
        return (a2 * cw_ref[0:1, cs] + a1 * cw_ref[1:2, cs] + a * cw_ref[2:3, cs]) + cb_ref[:, cs]

    for c in range(D_FF // FFN_CHUNK):
        gate = conv(c * FFN_CHUNK, 0)
        val = conv(D_FF + c * FFN_CHUNK, 1)
        hmid_ref[:, c * FFN_CHUNK:(c + 1) * FFN_CHUNK] = (
            jax.nn.gelu(gate, approximate=True) * val).astype(BF16)

    f = jnp.dot(hmid_ref[...], wdn_ref[...], preferred_element_type=F32)
    out_ref[...] = h + _rms(f, g3_ref[...])


def _conv_ffn(h3, layer, g_all, w_up, cw, cb, w_down):
    batch, seq, _ = h3.shape
    tm = ROW_TILE
    row = pl.BlockSpec((None, tm, D_MODEL), lambda b, t: (b, t, 0))
    return pl.pallas_call(
        functools.partial(_ffn_kernel, tm=tm),
        grid=(batch, seq // tm),
        in_specs=[row,
                  _resident((None, 1, D_MODEL), lambda b, t: (4 * layer + 2, 0, 0)),
                  _resident((None, D_MODEL, 2 * D_FF), lambda b, t: (layer, 0, 0)),
                  _resident((None, CONV_W, 2 * D_FF), lambda b, t: (layer, 0, 0)),
                  _resident((None, 1, 2 * D_FF), lambda b, t: (layer, 0, 0)),
                  _resident((None, D_FF, D_MODEL), lambda b, t: (layer, 0, 0)),
                  _resident((None, 1, D_MODEL), lambda b, t: (4 * layer + 3, 0, 0))],
        out_specs=row,
        out_shape=jax.ShapeDtypeStruct(h3.shape, F32),
        scratch_shapes=[pltpu.VMEM((SUBLANES, 2 * D_FF), F32),
                        pltpu.VMEM((2, tm + SUBLANES, FFN_CHUNK), F32),
                        pltpu.VMEM((tm, D_FF), BF16)],
        compiler_params=_params(2),
        name="conv_ffn",
    )(h3, g_all, w_up, cw, cb, w_down, g_all)


def _kvf_kernel(x_ref, g_ref, wkv_ref, wf_ref, bf_ref, k_ref, v_ref, c_ref, carry_ref, *, tm):
    @pl.when(pl.program_id(1) == 0)
    def _():
        carry_ref[...] = jnp.zeros_like(carry_ref)

    xn = _rms(x_ref[...], g_ref[...]).astype(BF16)
    kv = jnp.dot(xn, wkv_ref[...], preferred_element_type=F32)
    k_ref[...] = kv[:, :B_WIDTH].astype(BF16)
    v_ref[...] = kv[:, B_WIDTH:].astype(BF16)
    z = jnp.dot(xn, wf_ref[...], preferred_element_type=F32) + bf_ref[...]
    log_f = -(jnp.maximum(-z, 0.0) + jnp.log1p(jnp.exp(-jnp.abs(z))))
    rows = lax.broadcasted_iota(jnp.int32, (tm, LANES), 0)
    acc = log_f
    shift = 1
    while shift < tm:
        acc = acc + jnp.where(rows >= shift, pltpu.roll(acc, shift, 0), 0.0)
        shift *= 2
    acc = acc + carry_ref[...]
    c_ref[...] = acc
    carry_ref[...] = acc[tm - 1:tm, :]


def _kvf_proj(h3, g, wkv, wf, bf):
    batch, seq, _ = h3.shape
    tm = ROW_TILE
    row = lambda width: pl.BlockSpec((None, tm, width), lambda b, t: (b, t, 0))
    return pl.pallas_call(
        functools.partial(_kvf_kernel, tm=tm),
        grid=(batch, seq // tm),
        in_specs=[row(D_MODEL),
                  _resident((1, D_MODEL), lambda b, t: (0, 0)),
                  _resident((D_MODEL, 2 * B_WIDTH), lambda b, t: (0, 0)),
                  _resident((D_MODEL, LANES), lambda b, t: (0, 0)),
                  _resident((1, LANES), lambda b, t: (0, 0))],
        out_specs=[row(B_WIDTH), row(B_WIDTH), row(LANES)],
        out_shape=[jax.ShapeDtypeStruct((batch, seq, B_WIDTH), BF16),
                   jax.ShapeDtypeStruct((batch, seq, B_WIDTH), BF16),
                   jax.ShapeDtypeStruct((batch, seq, LANES), F32)],
        scratch_shapes=[pltpu.VMEM((1, LANES), F32)],
        compiler_params=_params(2),
        name="kvf_proj",
    )(h3, g, wkv, wf, bf)


def _qproj_kernel(x_ref, g_ref, w_ref, o_ref):
    xn = _rms(x_ref[...], g_ref[...]).astype(BF16)
    o_ref[...] = (jnp.dot(xn, w_ref[...], preferred_element_type=F32) * SCALE).astype(BF16)


def _q_proj(h2, g, w):
    m = h2.shape[0]
    tm = ROW_TILE
    return pl.pallas_call(
        _qproj_kernel,
        grid=(m // tm,),
        in_specs=[pl.BlockSpec((tm, D_MODEL), lambda i: (i, 0)),
                  _resident((1, D_MODEL), lambda i: (0, 0)),
                  _resident((D_MODEL, B_WIDTH), lambda i: (0, 0))],
        out_specs=pl.BlockSpec((tm, B_WIDTH), lambda i: (i, 0)),
        out_shape=jax.ShapeDtypeStruct((m, B_WIDTH), BF16),
        compiler_params=_params(1),
        name="q_proj",
    )(h2, g, w)


def _fox_kernel(q_ref, k_ref, v_ref, c_ref, ct_ref, o_ref, *, tq):
    hp = pl.program_id(1)
    qi = pl.program_id(2)
    q = q_ref[...]
    c_tile = c_ref[...]
    lane = lax.broadcasted_iota(jnp.int32, (tq, LANES), 1)
    causal = (lax.broadcasted_iota(jnp.int32, (tq, tq), 0)
              >= lax.broadcasted_iota(jnp.int32, (tq, tq), 1))
    outs = []
    for hh in range(LANES // HEAD_DIM):
        head = hp * (LANES // HEAD_DIM) + hh
        hmask = (lane >= hh * HEAD_DIM) & (lane < (hh + 1) * HEAD_DIM)
        qm = jnp.where(hmask, q, jnp.zeros_like(q))
        cq = jnp.sum(jnp.where(lane == head, c_tile, 0.0), axis=1, keepdims=True)

        def step(j, carry, diagonal):
            m, l, acc = carry
            k0 = pl.multiple_of(j * tq, tq)
            ks = k_ref[pl.ds(k0, tq), :]
            vs = v_ref[pl.ds(k0, tq), :]
            s = lax.dot_general(qm, ks, (((1,), (1,)), ((), ())), preferred_element_type=F32)
            s = s + (cq - ct_ref[head, j])
            if diagonal:
                s = jnp.where(causal, s, NEG)
            m_new = jnp.maximum(m, jnp.max(s, axis=-1, keepdims=True))
            alpha = jnp.exp(m - m_new)
            p = jnp.exp(s - m_new)
            l = alpha * l + jnp.sum(p, axis=-1, keepdims=True)
            acc = alpha * acc + jnp.dot(p.astype(BF16), vs, preferred_element_type=F32)
            return m_new, l, acc

        init = (jnp.full((tq, 1), NEG, F32), jnp.zeros((tq, 1), F32), jnp.zeros((tq, LANES), F32))
        carry = lax.fori_loop(0, qi, functools.partial(step, diagonal=False), init)
        _, l, acc = step(qi, carry, True)
        outs.append(acc / l)
    o_ref[...] = jnp.where(lane < HEAD_DIM, outs[0], outs[1]).astype(BF16)


def _fox_attention(q, k, v, c, c_t):
    batch, seq, _ = k.shape
    tq = FOX_TILE
    nt = seq // tq
    q3 = q.reshape(batch, seq, B_WIDTH)
    return pl.pallas_call(
        functools.partial(_fox_kernel, tq=tq),
        grid=(batch, B_WIDTH // LANES, nt),
        in_specs=[pl.BlockSpec((None, tq, LANES), lambda b, hp, i: (b, i, hp)),
                  pl.BlockSpec((None, seq, LANES), lambda b, hp, i: (b, 0, hp)),
                  pl.BlockSpec((None, seq, LANES), lambda b, hp, i: (b, 0, hp)),
                  pl.BlockSpec((None, tq, LANES), lambda b, hp, i: (b, i, 0)),
                  pl.BlockSpec((None, B_HEADS, nt, 1, tq), lambda b, hp, i: (b, 0, 0, 0, 0))],
        out_specs=pl.BlockSpec((None, tq, LANES), lambda b, hp, i: (b, i, hp)),
        out_shape=jax.ShapeDtypeStruct((batch, seq, B_WIDTH), BF16),
        compiler_params=_params(3),
        name="fox_attn",
    )(q3, k, v, c, c_t)


def _rope_lane_tables(seq):
    pos = jnp.arange(seq, dtype=F32)
    inv = ROPE_THETA ** (-jnp.arange(0, ROPE_DIM, 2, dtype=F32) / ROPE_DIM)
    ang = pos[:, None] * inv[None, :]
    cos, sin = jnp.cos(ang), jnp.sin(ang)
    ones = jnp.ones((seq, HEAD_DIM - ROPE_DIM), F32)
    zeros = jnp.zeros((seq, HEAD_DIM - ROPE_DIM), F32)
    zh = jnp.zeros((seq, ROPE_HALF), F32)
    cos_h = jnp.concatenate([cos, cos, ones], axis=1)
    sa_h = jnp.concatenate([zh, sin, zeros], axis=1)
    sb_h = jnp.concatenate([-sin, zh, zeros], axis=1)
    rep = LANES // HEAD_DIM
    return tuple(jnp.tile(t, (1, rep)) for t in (cos_h, sa_h, sb_h))


def kernel(x, norm_gains, w_qkv_a, w_o_a, w_q_b, w_o_b, kv_norm, w_kvf, b_f, w_up, conv_w, conv_b, w_down):
    batch, seq, _ = x.shape
    m = batch * seq
    cos_t, sa_t, sb_t = _rope_lane_tables(seq)
    gains = norm_gains.reshape(DEPTH * 4, 1, D_MODEL)
    w_qkv_a, w_o_a, w_q_b, w_o_b, w_up, w_down = (
        t.astype(BF16) for t in (w_qkv_a, w_o_a, w_q_b, w_o_b, w_up, w_down))
    w_kv = w_kvf[:, :2 * B_WIDTH].astype(BF16)
    w_f = jnp.pad(w_kvf[:, 2 * B_WIDTH:], ((0, 0), (0, LANES - B_HEADS))).astype(BF16)
    b_f = jnp.pad(b_f, (0, LANES - B_HEADS)).reshape(1, LANES)
    conv_b = conv_b.reshape(DEPTH, 1, 2 * D_FF)

    h = x.reshape(m, D_MODEL)
    k_sh = v_sh = c_sh = c_t = None
    for l in range(DEPTH):
        g = lambda i: gains[4 * l + i]
        if l < N_A:
            qkv = _qkv_proj(h, g(0), w_qkv_a[l], cos_t, sa_t, sb_t, seq)
            qkv = qkv.reshape(batch, seq, 3 * A_WIDTH)
            os_, ls_ = zip(*[_dilated_attention(qkv, gi, window, r, batch, seq)
                             for gi, (window, r) in enumerate(A_GROUPS)])
            h = _oproj_a(os_, ls_, w_o_a[l], h, g(1))
        else:
            if l == N_A:
                k_sh, v_sh, c_sh = _kvf_proj(h.reshape(batch, seq, D_MODEL), kv_norm.reshape(1, D_MODEL),
                                             w_kv, w_f, b_f)
                nt = seq // FOX_TILE
                c_t = c_sh[:, :, :B_HEADS].transpose(0, 2, 1).reshape(batch, B_HEADS, nt, 1, FOX_TILE)
            j = l - N_A
            q = _q_proj(h, g(0), w_q_b[j])
            o = _fox_attention(q, k_sh, v_sh, c_sh, c_t)
            h = _oproj_b(o.reshape(m, B_WIDTH), w_o_b[j], h, g(1))
        h = _conv_ffn(h.reshape(batch, seq, D_MODEL), l, gains, w_up, conv_w, conv_b, w_down)
        h = h.reshape(m, D_MODEL)
    return h.reshape(batch, seq, D_MODEL)
```

```python
import functools

import numpy as np

import jax
import jax.numpy as jnp
from jax import lax
from jax.experimental import pallas as pl
from jax.experimental.pallas import tpu as pltpu

D_MODEL = 1024
DEPTH = 4
HEAD_DIM = 64
A_GROUPS = ((128, 1), (512, 4), (2048, 16))
A_HEADS_PER_GROUP = 4
A_HEADS = A_HEADS_PER_GROUP * len(A_GROUPS)
A_WIDTH = A_HEADS * HEAD_DIM
GROUP_WIDTH = A_HEADS_PER_GROUP * HEAD_DIM
B_HEADS = D_MODEL // HEAD_DIM
B_WIDTH = B_HEADS * HEAD_DIM
N_A = DEPTH // 2
D_FF = 2816
CONV_W = 3
ROPE_DIM = HEAD_DIM // 4
ROPE_HALF = ROPE_DIM // 2
ROPE_THETA = 500000.0
BLK = 128
EPS = 1e-6
NEG = -1e30
SCALE = HEAD_DIM ** -0.5

LANES = 128
SUBLANES = 8
MXU_DIM = 256
VMEM_LIMIT = 56 * 1024 * 1024

ROW_TILE = 512
FFN_CHUNK = MXU_DIM
FOX_TILE = 512
AUG_WIDTH = B_HEADS * LANES
BIAS_PIECES = 3
DILATED_PLAN = ((1, 3), (4, 4), (16, 4))

F32 = jnp.float32
BF16 = jnp.bfloat16
NT_DIMS = (((1,), (1,)), ((), ()))


def _params(n_axes):
    return pltpu.CompilerParams(dimension_semantics=("arbitrary",) * n_axes,
                                vmem_limit_bytes=VMEM_LIMIT)


def _resident(shape, index_map):
    return pl.BlockSpec(shape, index_map, pipeline_mode=pl.Buffered(1))


def _rms(x, g):
    return x * lax.rsqrt(jnp.mean(x * x, axis=-1, keepdims=True) + EPS) * g


def _qkv_kernel(x_ref, g_ref, w_ref, cos_ref, sa_ref, sb_ref, o0_ref, o1_ref, o2_ref, ybuf_ref, *, tm):
    xn = _rms(x_ref[...], g_ref[...]).astype(BF16)
    y = jnp.dot(xn, w_ref[...], preferred_element_type=F32)
    cos, sa, sb = cos_ref[...], sa_ref[...], sb_ref[...]
    slabs_per_group = 3 * GROUP_WIDTH // LANES
    slabs_per_kind = GROUP_WIDTH // LANES
    for c in range(3 * A_WIDTH // LANES):
        grp, kind = c // slabs_per_group, (c % slabs_per_group) // slabs_per_kind
        t = y[:, c * LANES:(c + 1) * LANES]
        if kind < 2:
            t = t * cos + pltpu.roll(t, ROPE_HALF, 1) * sa + pltpu.roll(t, LANES - ROPE_HALF, 1) * sb
        if kind == 0:
            t = t * SCALE
        if grp == 0:
            o0_ref[0, :, c * LANES:(c + 1) * LANES] = t.astype(BF16)
        else:
            ybuf_ref[c - slabs_per_group] = t
    for grp, o_ref in ((1, o1_ref), (2, o2_ref)):
        r = A_GROUPS[grp][1]
        for j in range(r):
            for c in range(slabs_per_group):
                o_ref[j, :, c * LANES:(c + 1) * LANES] = ybuf_ref[
                    (grp - 1) * slabs_per_group + c, pl.ds(j, tm // r, stride=r), :].astype(BF16)


def _qkv_proj(h3, g, w, cos_t, sa_t, sb_t):
    batch, seq, _ = h3.shape
    tm = ROW_TILE
    n = 3 * A_WIDTH
    gw = 3 * GROUP_WIDTH
    tab = pl.BlockSpec((tm, LANES), lambda b, t: (t, 0))
    (_, r1), (_, r2) = A_GROUPS[1], A_GROUPS[2]
    return pl.pallas_call(
        functools.partial(_qkv_kernel, tm=tm),
        grid=(batch, seq // tm),
        in_specs=[pl.BlockSpec((None, tm, D_MODEL), lambda b, t: (b, t, 0)),
                  _resident((1, D_MODEL), lambda b, t: (0, 0)),
                  _resident((D_MODEL, n), lambda b, t: (0, 0)),
                  tab, tab, tab],
        out_specs=[pl.BlockSpec((None, 1, tm, gw), lambda b, t: (b, 0, t, 0)),
                   pl.BlockSpec((None, r1, tm // r1, gw), lambda b, t: (b, 0, t, 0)),
                   pl.BlockSpec((None, r2, tm // r2, gw), lambda b, t: (b, 0, t, 0))],
        out_shape=[jax.ShapeDtypeStruct((batch, 1, seq, gw), BF16),
                   jax.ShapeDtypeStruct((batch, r1, seq // r1, gw), BF16),
                   jax.ShapeDtypeStruct((batch, r2, seq // r2, gw), BF16)],
        scratch_shapes=[pltpu.VMEM((2 * gw // LANES, tm, LANES), F32)],
        compiler_params=_params(2),
        name="qkv_proj",
    )(h3, g, w, cos_t, sa_t, sb_t)


def _dilated_kernel(q_ref, k_ref, v_ref, o_ref, lse_ref, *, n_res, nb, n_back, unroll):
    nh = A_HEADS_PER_GROUP
    lane = lax.broadcasted_iota(jnp.int32, (BLK, GROUP_WIDTH), 1)
    head_masks = [(lane >= h * HEAD_DIM) & (lane < (h + 1) * HEAD_DIM) for h in range(nh)]

    def band(width):
        qi = lax.broadcasted_iota(jnp.int32, (nh * BLK, width), 0) & (BLK - 1)
        kj = lax.broadcasted_iota(jnp.int32, (nh * BLK, width), 1)
        rel = qi + (width - BLK) - kj
        return (rel >= 0) & (rel <= n_back)

    def attend(items, width):
        mask = band(width)
        scores = []
        for res, q0, k0 in items:
            q = q_ref[res, pl.ds(q0, BLK), :]
            qs = jnp.concatenate([jnp.where(hm, q, jnp.zeros_like(q)) for hm in head_masks], axis=0)
            scores.append(lax.dot_general(qs, k_ref[res, pl.ds(k0, width), :], NT_DIMS,
                                          preferred_element_type=F32))
        probs = []
        for s in scores:
            s = jnp.where(mask, s, NEG)
            m = jnp.max(s, axis=-1, keepdims=True)
            p = jnp.exp(s - m)
            l = jnp.sum(p, axis=-1, keepdims=True)
            probs.append(((p / l).astype(BF16), m + jnp.log(l)))
        for (res, q0, k0), (pn, lse) in zip(items, probs):
            pv = jnp.dot(pn, v_ref[res, pl.ds(k0, width), :], preferred_element_type=F32)
            o = pv[0:BLK]
            ls = jnp.broadcast_to(lse[0:BLK], (BLK, GROUP_WIDTH))
            for h in range(1, nh):
                o = jnp.where(head_masks[h], pv[h * BLK:(h + 1) * BLK], o)
                ls = jnp.where(head_masks[h], lse[h * BLK:(h + 1) * BLK], ls)
            o_ref[res, pl.ds(q0, BLK), :] = o.astype(BF16)
            lse_ref[res, pl.ds(q0, BLK), :] = ls

    if n_res <= unroll:
        attend([(res, 0, 0) for res in range(n_res)], BLK)
    else:
        def first(it, carry):
            attend([(it * unroll + u, 0, 0) for u in range(unroll)], BLK)
            return carry
        lax.fori_loop(0, n_res // unroll, first, 0)

    if nb > 1 and n_res == 1:
        def rest(it, carry):
            items = []
            for u in range(unroll):
                q0 = pl.multiple_of((1 + it * unroll + u) * BLK, BLK)
                items.append((0, q0, pl.multiple_of(q0 - BLK, BLK)))
            attend(items, 2 * BLK)
            return carry
        lax.fori_loop(0, (nb - 1) // unroll, rest, 0)
    elif nb > 1:
        def rest(i, carry):
            q0 = pl.multiple_of(i * BLK, BLK)
            attend([(res, q0, pl.multiple_of(q0 - BLK, BLK)) for res in range(n_res)], 2 * BLK)
            return carry
        lax.fori_loop(1, nb, rest, 0)


def _dilated_attention(qkv_g, g, window, r):
    batch, _, ln, _ = qkv_g.shape
    n_res, unroll = DILATED_PLAN[g]
    nb = ln // BLK
    assert ln % BLK == 0 and r % n_res == 0
    assert n_res % unroll == 0 or n_res < unroll
    assert n_res > 1 or (nb - 1) % unroll == 0
    assert n_res == 1 or n_res <= unroll or nb == 1

    def in_spec(which):
        return pl.BlockSpec((None, n_res, ln, GROUP_WIDTH), lambda b, j: (b, j, 0, which))

    out_spec = pl.BlockSpec((None, n_res, ln, GROUP_WIDTH), lambda b, j: (b, j, 0, 0))
    return pl.pallas_call(
        functools.partial(_dilated_kernel, n_res=n_res, nb=nb, n_back=window // r, unroll=unroll),
        grid=(batch, r // n_res),
        in_specs=[in_spec(0), in_spec(1), in_spec(2)],
        out_specs=[out_spec, out_spec],
        out_shape=[jax.ShapeDtypeStruct((batch, r, ln, GROUP_WIDTH), BF16),
                   jax.ShapeDtypeStruct((batch, r, ln, GROUP_WIDTH), F32)],
        compiler_params=_params(2),
        name=f"dilated_attn_g{g}",
    )(qkv_g, qkv_g, qkv_g)


def _oproj_a_kernel(o0_ref, o1_ref, o2_ref, l0_ref, l1_ref, l2_ref, w_ref, h_ref, g_ref, out_ref,
                    obuf_ref, lbuf_ref, *, tm):
    slabs = GROUP_WIDTH // LANES
    for grp, (o_ref, l_ref) in enumerate(((o0_ref, l0_ref), (o1_ref, l1_ref), (o2_ref, l2_ref))):
        r = A_GROUPS[grp][1]
        for j in range(r):
            rows = pl.ds(j, tm // r, stride=r) if r > 1 else slice(None)
            for c in range(slabs):
                cols = slice(c * LANES, (c + 1) * LANES)
                obuf_ref[grp * slabs + c, rows, :] = o_ref[j, :, cols].astype(F32)
                lbuf_ref[grp * slabs + c, rows, :] = l_ref[j, :, cols]

    def group(buf_ref, grp):
        return jnp.concatenate([buf_ref[grp * slabs + c] for c in range(slabs)], axis=1)

    l0, l1, l2 = (group(lbuf_ref, grp) for grp in range(3))
    mx = jnp.maximum(jnp.maximum(l0, l1), l2)
    e0, e1, e2 = jnp.exp(l0 - mx), jnp.exp(l1 - mx), jnp.exp(l2 - mx)
    den = e0 + e1 + e2
    mixed = jnp.concatenate([(group(obuf_ref, grp) * (e / den)).astype(BF16)
                             for grp, e in enumerate((e0, e1, e2))], axis=1)
    mix = jnp.dot(mixed, w_ref[...], preferred_element_type=F32)
    out_ref[...] = h_ref[...] + _rms(mix, g_ref[...])


def _oproj_a(os_, ls_, w, h3, g):
    batch, seq, _ = h3.shape
    tm = ROW_TILE
    row = pl.BlockSpec((None, tm, D_MODEL), lambda b, t: (b, t, 0))
    grp_specs = [pl.BlockSpec((None, r, tm // r, GROUP_WIDTH), lambda b, t: (b, 0, t, 0))
                 for _, r in A_GROUPS]
    return pl.pallas_call(
        functools.partial(_oproj_a_kernel, tm=tm),
        grid=(batch, seq // tm),
        in_specs=grp_specs + grp_specs + [_resident((A_WIDTH, D_MODEL), lambda b, t: (0, 0)), row,
                                          _resident((1, D_MODEL), lambda b, t: (0, 0))],
        out_specs=row,
        out_shape=jax.ShapeDtypeStruct(h3.shape, F32),
        scratch_shapes=[pltpu.VMEM((A_WIDTH // LANES, tm, LANES), F32),
                        pltpu.VMEM((A_WIDTH // LANES, tm, LANES), F32)],
        compiler_params=_params(2),
        name="oproj_a",
    )(*os_, *ls_, w, h3, g)


def _oproj_b_kernel(o_ref, w_ref, h_ref, g_ref, out_ref):
    mix = jnp.dot(o_ref[...], w_ref[...], preferred_element_type=F32)
    out_ref[...] = h_ref[...] + _rms(mix, g_ref[...])


def _oproj_b(o, w, h2, g):
    m = h2.shape[0]
    tm = ROW_TILE
    row = pl.BlockSpec((tm, D_MODEL), lambda i: (i, 0))
    return pl.pallas_call(
        _oproj_b_kernel,
        grid=(m // tm,),
        in_specs=[pl.BlockSpec((tm, B_WIDTH), lambda i: (i, 0)),
                  _resident((B_WIDTH, D_MODEL), lambda i: (0, 0)), row,
                  _resident((1, D_MODEL), lambda i: (0, 0))],
        out_specs=row,
        out_shape=jax.ShapeDtypeStruct((m, D_MODEL), F32),
        compiler_params=_params(1),
        name="oproj_b",
    )(o, w, h2, g)


def _ffn_kernel(h_ref, g2_ref, wup_ref, cw_ref, cb_ref, wdn_ref, g3_ref, out_ref,
                carry_ref, abuf_ref, hmid_ref, *, tm):
    @pl.when(pl.program_id(1) == 0)
    def _():
        carry_ref[...] = jnp.zeros_like(carry_ref)

    h = h_ref[...]
    xn = _rms(h, g2_ref[...]).astype(BF16)

    def conv(col, slot):
        cs = slice(col, col + FFN_CHUNK)
        a = jnp.dot(xn, wup_ref[:, cs], preferred_element_type=F32)
        abuf_ref[slot, 0:SUBLANES, :] = carry_ref[:, cs]
        abuf_ref[slot, SUBLANES:SUBLANES + tm, :] = a
        carry_ref[:, cs] = a[tm - SUBLANES:tm, :]
        a1 = abuf_ref[slot, SUBLANES - 1:SUBLANES - 1 + tm, :]
        a2 = abuf_ref[slot, SUBLANES - 2:SUBLANES - 2 + tm, :]
        return (a2 * cw_ref[0:1, cs] + a1 * cw_ref[1:2, cs] + a * cw_ref[2:3, cs]) + cb_ref[:, cs]

    for c in range(D_FF // FFN_CHUNK):
        gate = conv(c * FFN_CHUNK, 0)
        val = conv(D_FF + c * FFN_CHUNK, 1)
        hmid_ref[:, c * FFN_CHUNK:(c + 1) * FFN_CHUNK] = (
            jax.nn.gelu(gate, approximate=True) * val).astype(BF16)

    f = jnp.dot(hmid_ref[...], wdn_ref[...], preferred_element_type=F32)
    out_ref[...] = h + _rms(f, g3_ref[...])


def _conv_ffn(h3, layer, g_all, w_up, cw, cb, w_down):
    batch, seq, _ = h3.shape
    tm = ROW_TILE
    row = pl.BlockSpec((None, tm, D_MODEL), lambda b, t: (b, t, 0))
    return pl.pallas_call(
        functools.partial(_ffn_kernel, tm=tm),
        grid=(batch, seq // tm),
        in_specs=[row,
                  _resident((None, 1, D_MODEL), lambda b, t: (4 * layer + 2, 0, 0)),
                  _resident((None, D_MODEL, 2 * D_FF), lambda b, t: (layer, 0, 0)),
                  _resident((None, CONV_W, 2 * D_FF), lambda b, t: (layer, 0, 0)),
                  _resident((None, 1, 2 * D_FF), lambda b, t: (layer, 0, 0)),
                  _resident((None, D_FF, D_MODEL), lambda b, t: (layer, 0, 0)),
                  _resident((None, 1, D_MODEL), lambda b, t: (4 * layer + 3, 0, 0))],
        out_specs=row,
        out_shape=jax.ShapeDtypeStruct(h3.shape, F32),
        scratch_shapes=[pltpu.VMEM((SUBLANES, 2 * D_FF), F32),
                        pltpu.VMEM((2, tm + SUBLANES, FFN_CHUNK), F32),
                        pltpu.VMEM((tm, D_FF), BF16)],
        compiler_params=_params(2),
        name="conv_ffn",
    )(h3, g_all, w_up, cw, cb, w_down, g_all)


def _split_bf16(c):
    pieces, rest = [], c
    for _ in range(BIAS_PIECES):
        piece = rest.astype(BF16)
        pieces.append(piece)
        rest = rest - piece.astype(F32)
    return jnp.concatenate(pieces, axis=1)


def _store_augmented(x, c_tile, place_ref, ones_ref, out_ref):
    tm = x.shape[0]
    aug = jnp.dot(_split_bf16(c_tile), place_ref[...], preferred_element_type=F32) + ones_ref[...]
    own = lax.broadcasted_iota(jnp.int32, (tm, LANES), 1) < HEAD_DIM
    for h in range(B_HEADS):
        pair = x[:, (h // 2) * LANES:(h // 2 + 1) * LANES]
        if h % 2 == 1:
            pair = pltpu.roll(pair, HEAD_DIM, 1)
        out_ref[:, h * LANES:(h + 1) * LANES] = jnp.where(
            own, pair, aug[:, h * LANES:(h + 1) * LANES]).astype(BF16)


def _kvf_kernel(x_ref, g_ref, wkv_ref, wf_ref, bf_ref, place_ref, ones_ref, k_ref, v_ref, c_ref,
                carry_ref, *, tm):
    @pl.when(pl.program_id(1) == 0)
    def _():
        carry_ref[...] = jnp.zeros_like(carry_ref)

    xn = _rms(x_ref[...], g_ref[...]).astype(BF16)
    kv = jnp.dot(xn, wkv_ref[...], preferred_element_type=F32)
    v_ref[...] = kv[:, B_WIDTH:].astype(BF16)
    z = jnp.dot(xn, wf_ref[...], preferred_element_type=F32) + bf_ref[...]
    log_f = -(jnp.maximum(-z, 0.0) + jnp.log1p(jnp.exp(-jnp.abs(z))))
    rows = lax.broadcasted_iota(jnp.int32, (tm, LANES), 0)
    acc = log_f
    shift = 1
    while shift < tm:
        acc = acc + jnp.where(rows >= shift, pltpu.roll(acc, shift, 0), 0.0)
        shift *= 2
    acc = acc + carry_ref[...]
    c_ref[...] = acc
    carry_ref[...] = acc[tm - 1:tm, :]
    _store_augmented(kv[:, :B_WIDTH], acc, place_ref, ones_ref, k_ref)


def _kvf_proj(h3, g, wkv, wf, bf, place_k, ones_k):
    batch, seq, _ = h3.shape
    tm = ROW_TILE
    row = lambda width: pl.BlockSpec((None, tm, width), lambda b, t: (b, t, 0))
    return pl.pallas_call(
        functools.partial(_kvf_kernel, tm=tm),
        grid=(batch, seq // tm),
        in_specs=[row(D_MODEL),
                  _resident((1, D_MODEL), lambda b, t: (0, 0)),
                  _resident((D_MODEL, 2 * B_WIDTH), lambda b, t: (0, 0)),
                  _resident((D_MODEL, LANES), lambda b, t: (0, 0)),
                  _resident((1, LANES), lambda b, t: (0, 0)),
                  _resident((BIAS_PIECES * LANES, AUG_WIDTH), lambda b, t: (0, 0)),
                  _resident((1, AUG_WIDTH), lambda b, t: (0, 0))],
        out_specs=[row(AUG_WIDTH), row(B_WIDTH), row(LANES)],
        out_shape=[jax.ShapeDtypeStruct((batch, seq, AUG_WIDTH), BF16),
                   jax.ShapeDtypeStruct((batch, seq, B_WIDTH), BF16),
                   jax.ShapeDtypeStruct((batch, seq, LANES), F32)],
        scratch_shapes=[pltpu.VMEM((1, LANES), F32)],
        compiler_params=_params(2),
        name="kvf_proj",
    )(h3, g, wkv, wf, bf, place_k, ones_k)


def _qproj_kernel(x_ref, g_ref, w_ref, c_ref, place_ref, ones_ref, o_ref):
    xn = _rms(x_ref[...], g_ref[...]).astype(BF16)
    q = jnp.dot(xn, w_ref[...], preferred_element_type=F32) * SCALE
    _store_augmented(q, c_ref[...], place_ref, ones_ref, o_ref)


def _q_proj(h2, g, w, c2, place_q, ones_q):
    m = h2.shape[0]
    tm = ROW_TILE
    return pl.pallas_call(
        _qproj_kernel,
        grid=(m // tm,),
        in_specs=[pl.BlockSpec((tm, D_MODEL), lambda i: (i, 0)),
                  _resident((1, D_MODEL), lambda i: (0, 0)),
                  _resident((D_MODEL, B_WIDTH), lambda i: (0, 0)),
                  pl.BlockSpec((tm, LANES), lambda i: (i, 0)),
                  _resident((BIAS_PIECES * LANES, AUG_WIDTH), lambda i: (0, 0)),
                  _resident((1, AUG_WIDTH), lambda i: (0, 0))],
        out_specs=pl.BlockSpec((tm, AUG_WIDTH), lambda i: (i, 0)),
        out_shape=jax.ShapeDtypeStruct((m, AUG_WIDTH), BF16),
        compiler_params=_params(1),
        name="q_proj",
    )(h2, g, w, c2, place_q, ones_q)


def _fox_kernel(q_ref, k_ref, v_ref, o_ref, *, tq):
    qi = pl.program_id(2)
    nh = LANES // HEAD_DIM
    causal = (lax.broadcasted_iota(jnp.int32, (tq, tq), 0)
              >= lax.broadcasted_iota(jnp.int32, (tq, tq), 1))

    def step(j, carry, diagonal):
        k0 = pl.multiple_of(j * tq, tq)
        scores = []
        for hh in range(nh):
            cols = slice(hh * LANES, (hh + 1) * LANES)
            scores.append(lax.dot_general(q_ref[:, cols], k_ref[pl.ds(k0, tq), cols], NT_DIMS,
                                          preferred_element_type=F32))
        stats = []
        for hh in range(nh):
            m, l, _ = carry[hh]
            s = scores[hh]
            if diagonal:
                s = jnp.where(causal, s, NEG)
            m_new = jnp.maximum(m, jnp.max(s, axis=-1, keepdims=True))
            alpha = jnp.exp(m - m_new)
            p = jnp.exp(s - m_new)
            l_new = alpha * l + jnp.sum(p, axis=-1, keepdims=True)
            stats.append((m_new, l_new, alpha, p.astype(BF16)))
        out = []
        for hh in range(nh):
            m_new, l_new, alpha, p = stats[hh]
            acc = alpha * carry[hh][2] + jnp.dot(p, v_ref[pl.ds(k0, tq), :], preferred_element_type=F32)
            out.append((m_new, l_new, acc))
        return tuple(out)

    init = tuple((jnp.full((tq, 1), NEG, F32), jnp.zeros((tq, 1), F32), jnp.zeros((tq, LANES), F32))
                 for _ in range(nh))
    carry = lax.fori_loop(0, qi, functools.partial(step, diagonal=False), init)
    final = step(qi, carry, True)
    outs = [acc / l for _, l, acc in final]
    lane = lax.broadcasted_iota(jnp.int32, (tq, LANES), 1)
    o_ref[...] = jnp.where(lane < HEAD_DIM, outs[0], outs[1]).astype(BF16)


def _fox_attention(q_aug, k_aug, v):
    batch, seq, _ = v.shape
    tq = FOX_TILE
    nh = LANES // HEAD_DIM
    q3 = q_aug.reshape(batch, seq, AUG_WIDTH)
    return pl.pallas_call(
        functools.partial(_fox_kernel, tq=tq),
        grid=(batch, B_WIDTH // LANES, seq // tq),
        in_specs=[pl.BlockSpec((None, tq, nh * LANES), lambda b, hp, i: (b, i, hp)),
                  pl.BlockSpec((None, seq, nh * LANES), lambda b, hp, i: (b, 0, hp)),
                  pl.BlockSpec((None, seq, LANES), lambda b, hp, i: (b, 0, hp))],
        out_specs=pl.BlockSpec((None, tq, LANES), lambda b, hp, i: (b, i, hp)),
        out_shape=jax.ShapeDtypeStruct((batch, seq, B_WIDTH), BF16),
        compiler_params=_params(3),
        name="fox_attn",
    )(q3, k_aug, v)


def _bias_placement():
    place_q = np.zeros((BIAS_PIECES * LANES, AUG_WIDTH), np.float32)
    place_k = np.zeros((BIAS_PIECES * LANES, AUG_WIDTH), np.float32)
    ones_q = np.zeros((1, AUG_WIDTH), np.float32)
    ones_k = np.zeros((1, AUG_WIDTH), np.float32)
    for h in range(B_HEADS):
        base = h * LANES + HEAD_DIM
        for piece in range(BIAS_PIECES):
            place_q[piece * LANES + h, base + piece] = 1.0
            ones_k[0, base + piece] = 1.0
            place_k[piece * LANES + h, base + BIAS_PIECES + piece] = -1.0
            ones_q[0, base + BIAS_PIECES + piece] = 1.0
    return (jnp.asarray(place_q, BF16), jnp.asarray(ones_q), jnp.asarray(place_k, BF16), jnp.asarray(ones_k))


def _rope_lane_tables(seq):
    pos = jnp.arange(seq, dtype=F32)
    inv = ROPE_THETA ** (-jnp.arange(0, ROPE_DIM, 2, dtype=F32) / ROPE_DIM)
    ang = pos[:, None] * inv[None, :]
    cos, sin = jnp.cos(ang), jnp.sin(ang)
    ones = jnp.ones((seq, HEAD_DIM - ROPE_DIM), F32)
    zeros = jnp.zeros((seq, HEAD_DIM - ROPE_DIM), F32)
    zh = jnp.zeros((seq, ROPE_HALF), F32)
    cos_h = jnp.concatenate([cos, cos, ones], axis=1)
    sa_h = jnp.concatenate([zh, sin, zeros], axis=1)
    sb_h = jnp.concatenate([-sin, zh, zeros], axis=1)
    rep = LANES // HEAD_DIM
    return tuple(jnp.tile(t, (1, rep)) for t in (cos_h, sa_h, sb_h))


def _group_major(w_qkv):
    d = w_qkv.shape[0]
    w = w_qkv.reshape(d, 3, len(A_GROUPS), GROUP_WIDTH).transpose(0, 2, 1, 3)
    return w.reshape(d, 3 * A_WIDTH)


def kernel(x, norm_gains, w_qkv_a, w_o_a, w_q_b, w_o_b, kv_norm, w_kvf, b_f, w_up, conv_w, conv_b, w_down):
    batch, seq, _ = x.shape
    m = batch * seq
    cos_t, sa_t, sb_t = _rope_lane_tables(seq)
    place_q, ones_q, place_k, ones_k = _bias_placement()
    gains = norm_gains.reshape(DEPTH * 4, 1, D_MODEL)
    w_qkv_a, w_o_a, w_q_b, w_o_b, w_up, w_down = (
        t.astype(BF16) for t in (w_qkv_a, w_o_a, w_q_b, w_o_b, w_up, w_down))
    w_kv = w_kvf[:, :2 * B_WIDTH].astype(BF16)
    w_f = jnp.pad(w_kvf[:, 2 * B_WIDTH:], ((0, 0), (0, LANES - B_HEADS))).astype(BF16)
    b_f = jnp.pad(b_f, (0, LANES - B_HEADS)).reshape(1, LANES)
    conv_b = conv_b.reshape(DEPTH, 1, 2 * D_FF)

    h = x
    k_aug = v_sh = c_sh = None
    for l in range(DEPTH):
        g = lambda i: gains[4 * l + i]
        if l < N_A:
            qkv = _qkv_proj(h, g(0), _group_major(w_qkv_a[l]), cos_t, sa_t, sb_t)
            os_, ls_ = zip(*[_dilated_attention(qkv[gi], gi, window, r)
                             for gi, (window, r) in enumerate(A_GROUPS)])
            h = _oproj_a(os_, ls_, w_o_a[l], h, g(1))
        else:
            if l == N_A:
                k_aug, v_sh, c_sh = _kvf_proj(h, kv_norm.reshape(1, D_MODEL), w_kv, w_f, b_f,
                                              place_k, ones_k)
            j = l - N_A
            q_aug = _q_proj(h.reshape(m, D_MODEL), g(0), w_q_b[j], c_sh.reshape(m, LANES),
                            place_q, ones_q)
            o = _fox_attention(q_aug, k_aug, v_sh)
            h = _oproj_b(o.reshape(m, B_WIDTH), w_o_b[j], h.reshape(m, D_MODEL), g(1))
            h = h.reshape(batch, seq, D_MODEL)
        h = _conv_ffn(h, l, gains, w_up, conv_w, conv_b, w_down)
    return h
```

```python
import functools

import numpy as np

import jax
import jax.numpy as jnp
from jax import lax
from jax.experimental import pallas as pl
from jax.experimental.pallas import tpu as pltpu

D_MODEL = 1024
DEPTH = 4
HEAD_DIM = 64
A_GROUPS = ((128, 1), (512, 4), (2048, 16))
A_HEADS_PER_GROUP = 4
A_HEADS = A_HEADS_PER_GROUP * len(A_GROUPS)
A_WIDTH = A_HEADS * HEAD_DIM
GROUP_WIDTH = A_HEADS_PER_GROUP * HEAD_DIM
B_HEADS = D_MODEL // HEAD_DIM
B_WIDTH = B_HEADS * HEAD_DIM
N_A = DEPTH // 2
D_FF = 2816
CONV_W = 3
ROPE_DIM = HEAD_DIM // 4
ROPE_HALF = ROPE_DIM // 2
ROPE_THETA = 500000.0
BLK = 128
EPS = 1e-6
NEG = -1e30
SCALE = HEAD_DIM ** -0.5

LANES = 128
SUBLANES = 8
MXU_DIM = 256
VMEM_LIMIT = 56 * 1024 * 1024

ROW_TILE = 512
FFN_CHUNK = MXU_DIM
FOX_TILE = 512
FOX_HEADS = 4
AUG_WIDTH = B_HEADS * LANES
BIAS_PIECES = 3
BIAS_STRIDE = LANES // B_HEADS
DILATED_PLAN = ((1, 3), (4, 4), (16, 4))

F32 = jnp.float32
BF16 = jnp.bfloat16
NT_DIMS = (((1,), (1,)), ((), ()))


def _params(n_axes):
    return pltpu.CompilerParams(dimension_semantics=("arbitrary",) * n_axes,
                                vmem_limit_bytes=VMEM_LIMIT)


def _resident(shape, index_map):
    return pl.BlockSpec(shape, index_map, pipeline_mode=pl.Buffered(1))


def _rms(x, g):
    return x * lax.rsqrt(jnp.mean(x * x, axis=-1, keepdims=True) + EPS) * g


def _qkv_kernel(x_ref, g_ref, w_ref, cos_ref, sa_ref, sb_ref, o0_ref, o1_ref, o2_ref, ybuf_ref, *, tm):
    xn = _rms(x_ref[...], g_ref[...]).astype(BF16)
    y = jnp.dot(xn, w_ref[...], preferred_element_type=F32)
    cos, sa, sb = cos_ref[...], sa_ref[...], sb_ref[...]
    slabs_per_group = 3 * GROUP_WIDTH // LANES
    slabs_per_kind = GROUP_WIDTH // LANES
    for c in range(3 * A_WIDTH // LANES):
        grp, kind = c // slabs_per_group, (c % slabs_per_group) // slabs_per_kind
        t = y[:, c * LANES:(c + 1) * LANES]
        if kind < 2:
            t = t * cos + pltpu.roll(t, ROPE_HALF, 1) * sa + pltpu.roll(t, LANES - ROPE_HALF, 1) * sb
        if kind == 0:
            t = t * SCALE
        if grp == 0:
            o0_ref[0, :, c * LANES:(c + 1) * LANES] = t.astype(BF16)
        else:
            ybuf_ref[c - slabs_per_group] = t
    for grp, o_ref in ((1, o1_ref), (2, o2_ref)):
        r = A_GROUPS[grp][1]
        for j in range(r):
            for c in range(slabs_per_group):
                o_ref[j, :, c * LANES:(c + 1) * LANES] = ybuf_ref[
                    (grp - 1) * slabs_per_group + c, pl.ds(j, tm // r, stride=r), :].astype(BF16)


def _qkv_proj(h3, g, w, cos_t, sa_t, sb_t):
    batch, seq, _ = h3.shape
    tm = ROW_TILE
    n = 3 * A_WIDTH
    gw = 3 * GROUP_WIDTH
    tab = pl.BlockSpec((tm, LANES), lambda b, t: (t, 0))
    (_, r1), (_, r2) = A_GROUPS[1], A_GROUPS[2]
    return pl.pallas_call(
        functools.partial(_qkv_kernel, tm=tm),
        grid=(batch, seq // tm),
        in_specs=[pl.BlockSpec((None, tm, D_MODEL), lambda b, t: (b, t, 0)),
                  _resident((1, D_MODEL), lambda b, t: (0, 0)),
                  _resident((D_MODEL, n), lambda b, t: (0, 0)),
                  tab, tab, tab],
        out_specs=[pl.BlockSpec((None, 1, tm, gw), lambda b, t: (b, 0, t, 0)),
                   pl.BlockSpec((None, r1, tm // r1, gw), lambda b, t: (b, 0, t, 0)),
                   pl.BlockSpec((None, r2, tm // r2, gw), lambda b, t: (b, 0, t, 0))],
        out_shape=[jax.ShapeDtypeStruct((batch, 1, seq, gw), BF16),
                   jax.ShapeDtypeStruct((batch, r1, seq // r1, gw), BF16),
                   jax.ShapeDtypeStruct((batch, r2, seq // r2, gw), BF16)],
        scratch_shapes=[pltpu.VMEM((2 * gw // LANES, tm, LANES), F32)],
        compiler_params=_params(2),
        name="qkv_proj",
    )(h3, g, w, cos_t, sa_t, sb_t)


def _dilated_kernel(q_ref, k_ref, v_ref, o_ref, lse_ref, *, n_res, nb, n_back, unroll):
    nh = A_HEADS_PER_GROUP
    lane = lax.broadcasted_iota(jnp.int32, (BLK, GROUP_WIDTH), 1)
    head_masks = [(lane >= h * HEAD_DIM) & (lane < (h + 1) * HEAD_DIM) for h in range(nh)]

    def band(width):
        qi = lax.broadcasted_iota(jnp.int32, (nh * BLK, width), 0) & (BLK - 1)
        kj = lax.broadcasted_iota(jnp.int32, (nh * BLK, width), 1)
        rel = qi + (width - BLK) - kj
        return (rel >= 0) & (rel <= n_back)

    def attend(items, width):
        mask = band(width)
        scores = []
        for res, q0, k0 in items:
            q = q_ref[res, pl.ds(q0, BLK), :]
            qs = jnp.concatenate([jnp.where(hm, q, jnp.zeros_like(q)) for hm in head_masks], axis=0)
            scores.append(lax.dot_general(qs, k_ref[res, pl.ds(k0, width), :], NT_DIMS,
                                          preferred_element_type=F32))
        probs = []
        for s in scores:
            s = jnp.where(mask, s, NEG)
            m = jnp.max(s, axis=-1, keepdims=True)
            p = jnp.exp(s - m)
            l = jnp.sum(p, axis=-1, keepdims=True)
            probs.append(((p / l).astype(BF16), m + jnp.log(l)))
        for (res, q0, k0), (pn, lse) in zip(items, probs):
            pv = jnp.dot(pn, v_ref[res, pl.ds(k0, width), :], preferred_element_type=F32)
            o = pv[0:BLK]
            ls = jnp.broadcast_to(lse[0:BLK], (BLK, GROUP_WIDTH))
            for h in range(1, nh):
                o = jnp.where(head_masks[h], pv[h * BLK:(h + 1) * BLK], o)
                ls = jnp.where(head_masks[h], lse[h * BLK:(h + 1) * BLK], ls)
            o_ref[res, pl.ds(q0, BLK), :] = o.astype(BF16)
            lse_ref[res, pl.ds(q0, BLK), :] = ls

    if n_res <= unroll:
        attend([(res, 0, 0) for res in range(n_res)], BLK)
    else:
        def first(it, carry):
            attend([(it * unroll + u, 0, 0) for u in range(unroll)], BLK)
            return carry
        lax.fori_loop(0, n_res // unroll, first, 0)

    if nb > 1 and n_res == 1:
        def rest(it, carry):
            items = []
            for u in range(unroll):
                q0 = pl.multiple_of((1 + it * unroll + u) * BLK, BLK)
                items.append((0, q0, pl.multiple_of(q0 - BLK, BLK)))
            attend(items, 2 * BLK)
            return carry
        lax.fori_loop(0, (nb - 1) // unroll, rest, 0)
    elif nb > 1:
        def rest(i, carry):
            q0 = pl.multiple_of(i * BLK, BLK)
            attend([(res, q0, pl.multiple_of(q0 - BLK, BLK)) for res in range(n_res)], 2 * BLK)
            return carry
        lax.fori_loop(1, nb, rest, 0)


def _dilated_attention(qkv_g, g, window, r):
    batch, _, ln, _ = qkv_g.shape
    n_res, unroll = DILATED_PLAN[g]
    nb = ln // BLK
    assert ln % BLK == 0 and r % n_res == 0
    assert n_res % unroll == 0 or n_res < unroll
    assert n_res > 1 or (nb - 1) % unroll == 0
    assert n_res == 1 or n_res <= unroll or nb == 1

    def in_spec(which):
        return pl.BlockSpec((None, n_res, ln, GROUP_WIDTH), lambda b, j: (b, j, 0, which))

    out_spec = pl.BlockSpec((None, n_res, ln, GROUP_WIDTH), lambda b, j: (b, j, 0, 0))
    return pl.pallas_call(
        functools.partial(_dilated_kernel, n_res=n_res, nb=nb, n_back=window // r, unroll=unroll),
        grid=(batch, r // n_res),
        in_specs=[in_spec(0), in_spec(1), in_spec(2)],
        out_specs=[out_spec, out_spec],
        out_shape=[jax.ShapeDtypeStruct((batch, r, ln, GROUP_WIDTH), BF16),
                   jax.ShapeDtypeStruct((batch, r, ln, GROUP_WIDTH), F32)],
        compiler_params=_params(2),
        name=f"dilated_attn_g{g}",
    )(qkv_g, qkv_g, qkv_g)


def _oproj_a_kernel(o0_ref, o1_ref, o2_ref, l0_ref, l1_ref, l2_ref, w_ref, h_ref, g_ref, out_ref,
                    obuf_ref, lbuf_ref, *, tm):
    slabs = GROUP_WIDTH // LANES
    for grp, (o_ref, l_ref) in enumerate(((o0_ref, l0_ref), (o1_ref, l1_ref), (o2_ref, l2_ref))):
        r = A_GROUPS[grp][1]
        for j in range(r):
            rows = pl.ds(j, tm // r, stride=r) if r > 1 else slice(None)
            for c in range(slabs):
                cols = slice(c * LANES, (c + 1) * LANES)
                obuf_ref[grp * slabs + c, rows, :] = o_ref[j, :, cols].astype(F32)
                lbuf_ref[grp * slabs + c, rows, :] = l_ref[j, :, cols]

    def group(buf_ref, grp):
        return jnp.concatenate([buf_ref[grp * slabs + c] for c in range(slabs)], axis=1)

    l0, l1, l2 = (group(lbuf_ref, grp) for grp in range(3))
    mx = jnp.maximum(jnp.maximum(l0, l1), l2)
    e0, e1, e2 = jnp.exp(l0 - mx), jnp.exp(l1 - mx), jnp.exp(l2 - mx)
    den = e0 + e1 + e2
    mixed = jnp.concatenate([(group(obuf_ref, grp) * (e / den)).astype(BF16)
                             for grp, e in enumerate((e0, e1, e2))], axis=1)
    mix = jnp.dot(mixed, w_ref[...], preferred_element_type=F32)
    out_ref[...] = h_ref[...] + _rms(mix, g_ref[...])


def _oproj_a(os_, ls_, w, h3, g):
    batch, seq, _ = h3.shape
    tm = ROW_TILE
    row = pl.BlockSpec((None, tm, D_MODEL), lambda b, t: (b, t, 0))
    grp_specs = [pl.BlockSpec((None, r, tm // r, GROUP_WIDTH), lambda b, t: (b, 0, t, 0))
                 for _, r in A_GROUPS]
    return pl.pallas_call(
        functools.partial(_oproj_a_kernel, tm=tm),
        grid=(batch, seq // tm),
        in_specs=grp_specs + grp_specs + [_resident((A_WIDTH, D_MODEL), lambda b, t: (0, 0)), row,
                                          _resident((1, D_MODEL), lambda b, t: (0, 0))],
        out_specs=row,
        out_shape=jax.ShapeDtypeStruct(h3.shape, F32),
        scratch_shapes=[pltpu.VMEM((A_WIDTH // LANES, tm, LANES), F32),
                        pltpu.VMEM((A_WIDTH // LANES, tm, LANES), F32)],
        compiler_params=_params(2),
        name="oproj_a",
    )(*os_, *ls_, w, h3, g)


def _oproj_b_kernel(o_ref, w_ref, h_ref, g_ref, out_ref):
    mix = jnp.dot(o_ref[...], w_ref[...], preferred_element_type=F32)
    out_ref[...] = h_ref[...] + _rms(mix, g_ref[...])


def _oproj_b(o, w, h2, g):
    m = h2.shape[0]
    tm = ROW_TILE
    row = pl.BlockSpec((tm, D_MODEL), lambda i: (i, 0))
    return pl.pallas_call(
        _oproj_b_kernel,
        grid=(m // tm,),
        in_specs=[pl.BlockSpec((tm, B_WIDTH), lambda i: (i, 0)),
                  _resident((B_WIDTH, D_MODEL), lambda i: (0, 0)), row,
                  _resident((1, D_MODEL), lambda i: (0, 0))],
        out_specs=row,
        out_shape=jax.ShapeDtypeStruct((m, D_MODEL), F32),
        compiler_params=_params(1),
        name="oproj_b",
    )(o, w, h2, g)


def _ffn_kernel(h_ref, g2_ref, wup_ref, cw_ref, cb_ref, wdn_ref, g3_ref, out_ref,
                carry_ref, abuf_ref, hmid_ref, *, tm):
    @pl.when(pl.program_id(1) == 0)
    def _():
        carry_ref[...] = jnp.zeros_like(carry_ref)

    h = h_ref[...]
    xn = _rms(h, g2_ref[...]).astype(BF16)

    def conv(col, slot):
        cs = slice(col, col + FFN_CHUNK)
        a = jnp.dot(xn, wup_ref[:, cs], preferred_element_type=F32)
        abuf_ref[slot, 0:SUBLANES, :] = carry_ref[:, cs]
        abuf_ref[slot, SUBLANES:SUBLANES + tm, :] = a
        carry_ref[:, cs] = a[tm - SUBLANES:tm, :]
        a1 = abuf_ref[slot, SUBLANES - 1:SUBLANES - 1 + tm, :]
        a2 = abuf_ref[slot, SUBLANES - 2:SUBLANES - 2 + tm, :]
        return (a2 * cw_ref[0:1, cs] + a1 * cw_ref[1:2, cs] + a * cw_ref[2:3, cs]) + cb_ref[:, cs]

    for c in range(D_FF // FFN_CHUNK):
        gate = conv(c * FFN_CHUNK, 0)
        val = conv(D_FF + c * FFN_CHUNK, 1)
        hmid_ref[:, c * FFN_CHUNK:(c + 1) * FFN_CHUNK] = (
            jax.nn.gelu(gate, approximate=True) * val).astype(BF16)

    f = jnp.dot(hmid_ref[...], wdn_ref[...], preferred_element_type=F32)
    out_ref[...] = h + _rms(f, g3_ref[...])


def _conv_ffn(h3, layer, g_all, w_up, cw, cb, w_down):
    batch, seq, _ = h3.shape
    tm = ROW_TILE
    row = pl.BlockSpec((None, tm, D_MODEL), lambda b, t: (b, t, 0))
    return pl.pallas_call(
        functools.partial(_ffn_kernel, tm=tm),
        grid=(batch, seq // tm),
        in_specs=[row,
                  _resident((None, 1, D_MODEL), lambda b, t: (4 * layer + 2, 0, 0)),
                  _resident((None, D_MODEL, 2 * D_FF), lambda b, t: (layer, 0, 0)),
                  _resident((None, CONV_W, 2 * D_FF), lambda b, t: (layer, 0, 0)),
                  _resident((None, 1, 2 * D_FF), lambda b, t: (layer, 0, 0)),
                  _resident((None, D_FF, D_MODEL), lambda b, t: (layer, 0, 0)),
                  _resident((None, 1, D_MODEL), lambda b, t: (4 * layer + 3, 0, 0))],
        out_specs=row,
        out_shape=jax.ShapeDtypeStruct(h3.shape, F32),
        scratch_shapes=[pltpu.VMEM((SUBLANES, 2 * D_FF), F32),
                        pltpu.VMEM((2, tm + SUBLANES, FFN_CHUNK), F32),
                        pltpu.VMEM((tm, D_FF), BF16)],
        compiler_params=_params(2),
        name="conv_ffn",
    )(h3, g_all, w_up, cw, cb, w_down, g_all)


def _split_bf16(c):
    pieces, rest = [], c
    for _ in range(BIAS_PIECES):
        piece = rest.astype(BF16)
        pieces.append(piece)
        rest = rest - piece.astype(F32)
    return jnp.concatenate(pieces, axis=1)


def _store_augmented(x, c_tile, place_ref, ones_ref, out_ref):
    tm = x.shape[0]
    slab = jnp.dot(_split_bf16(c_tile), place_ref[...], preferred_element_type=F32) + ones_ref[...]
    lane = lax.broadcasted_iota(jnp.int32, (tm, LANES), 1)
    own = lane < HEAD_DIM
    bias_lanes = lane < HEAD_DIM + 2 * BIAS_PIECES
    for h in range(B_HEADS):
        pair = x[:, (h // 2) * LANES:(h // 2 + 1) * LANES]
        if h % 2 == 1:
            pair = pltpu.roll(pair, HEAD_DIM, 1)
        shift = (HEAD_DIM - BIAS_STRIDE * h) % LANES
        bias = pltpu.roll(slab, shift, 1) if shift else slab
        out_ref[:, h * LANES:(h + 1) * LANES] = jnp.where(
            own, pair, jnp.where(bias_lanes, bias, 0.0)).astype(BF16)


def _kvf_kernel(x_ref, g_ref, wkv_ref, wf_ref, bf_ref, place_ref, ones_ref, k_ref, v_ref, c_ref,
                carry_ref, *, tm):
    @pl.when(pl.program_id(1) == 0)
    def _():
        carry_ref[...] = jnp.zeros_like(carry_ref)

    xn = _rms(x_ref[...], g_ref[...]).astype(BF16)
    kv = jnp.dot(xn, wkv_ref[...], preferred_element_type=F32)
    v_ref[...] = kv[:, B_WIDTH:].T.astype(BF16)
    z = jnp.dot(xn, wf_ref[...], preferred_element_type=F32) + bf_ref[...]
    log_f = -(jnp.maximum(-z, 0.0) + jnp.log1p(jnp.exp(-jnp.abs(z))))
    rows = lax.broadcasted_iota(jnp.int32, (tm, LANES), 0)
    acc = log_f
    shift = 1
    while shift < tm:
        acc = acc + jnp.where(rows >= shift, pltpu.roll(acc, shift, 0), 0.0)
        shift *= 2
    acc = acc + carry_ref[...]
    c_ref[...] = acc
    carry_ref[...] = acc[tm - 1:tm, :]
    _store_augmented(kv[:, :B_WIDTH], acc, place_ref, ones_ref, k_ref)


def _kvf_proj(h3, g, wkv, wf, bf, place_k, ones_k):
    batch, seq, _ = h3.shape
    tm = ROW_TILE
    row = lambda width: pl.BlockSpec((None, tm, width), lambda b, t: (b, t, 0))
    return pl.pallas_call(
        functools.partial(_kvf_kernel, tm=tm),
        grid=(batch, seq // tm),
        in_specs=[row(D_MODEL),
                  _resident((1, D_MODEL), lambda b, t: (0, 0)),
                  _resident((D_MODEL, 2 * B_WIDTH), lambda b, t: (0, 0)),
                  _resident((D_MODEL, LANES), lambda b, t: (0, 0)),
                  _resident((1, LANES), lambda b, t: (0, 0)),
                  _resident((BIAS_PIECES * LANES, LANES), lambda b, t: (0, 0)),
                  _resident((1, LANES), lambda b, t: (0, 0))],
        out_specs=[row(AUG_WIDTH),
                   pl.BlockSpec((None, None, B_WIDTH, tm), lambda b, t: (b, t, 0, 0)),
                   row(LANES)],
        out_shape=[jax.ShapeDtypeStruct((batch, seq, AUG_WIDTH), BF16),
                   jax.ShapeDtypeStruct((batch, seq // tm, B_WIDTH, tm), BF16),
                   jax.ShapeDtypeStruct((batch, seq, LANES), F32)],
        scratch_shapes=[pltpu.VMEM((1, LANES), F32)],
        compiler_params=_params(2),
        name="kvf_proj",
    )(h3, g, wkv, wf, bf, place_k, ones_k)


def _qproj_kernel(x_ref, g_ref, w_ref, c_ref, place_ref, ones_ref, o_ref):
    xn = _rms(x_ref[...], g_ref[...]).astype(BF16)
    q = jnp.dot(xn, w_ref[...], preferred_element_type=F32) * SCALE
    _store_augmented(q, c_ref[...], place_ref, ones_ref, o_ref)


def _q_proj(h2, g, w, c2, place_q, ones_q):
    m = h2.shape[0]
    tm = ROW_TILE
    return pl.pallas_call(
        _qproj_kernel,
        grid=(m // tm,),
        in_specs=[pl.BlockSpec((tm, D_MODEL), lambda i: (i, 0)),
                  _resident((1, D_MODEL), lambda i: (0, 0)),
                  _resident((D_MODEL, B_WIDTH), lambda i: (0, 0)),
                  pl.BlockSpec((tm, LANES), lambda i: (i, 0)),
                  _resident((BIAS_PIECES * LANES, LANES), lambda i: (0, 0)),
                  _resident((1, LANES), lambda i: (0, 0))],
        out_specs=pl.BlockSpec((tm, AUG_WIDTH), lambda i: (i, 0)),
        out_shape=jax.ShapeDtypeStruct((m, AUG_WIDTH), BF16),
        compiler_params=_params(1),
        name="q_proj",
    )(h2, g, w, c2, place_q, ones_q)


def _fox_kernel(q_ref, k_ref, vt_ref, o_ref, *, tq, nh):
    qi = pl.program_id(2)

    def update(j, state, key0, nkeys, query0, diagonal):
        nq = tq - query0
        k0 = pl.multiple_of(j * tq, tq) + key0
        scores = []
        for hh in range(nh):
            cols = slice(hh * LANES, (hh + 1) * LANES)
            scores.append(lax.dot_general(k_ref[pl.ds(k0, nkeys), cols], q_ref[query0:, cols], NT_DIMS,
                                          preferred_element_type=F32))
        stats = []
        for hh in range(nh):
            m, l, _ = state[hh]
            s = scores[hh]
            if diagonal:
                visible = (lax.broadcasted_iota(jnp.int32, (nkeys, nq), 0)
                           <= lax.broadcasted_iota(jnp.int32, (nkeys, nq), 1))
                s = jnp.where(visible, s, NEG)
            m_new = jnp.maximum(m, jnp.max(s, axis=0, keepdims=True))
            alpha = jnp.exp(m - m_new)
            p = jnp.exp(s - m_new)
            l_new = alpha * l + jnp.sum(p, axis=0, keepdims=True)
            stats.append((m_new, l_new, alpha, p.astype(BF16)))
        out = []
        for hh in range(nh):
            m_new, l_new, alpha, p = stats[hh]
            vt = vt_ref[j, hh * HEAD_DIM:(hh + 1) * HEAD_DIM, key0:key0 + nkeys]
            acc = alpha * state[hh][2] + jnp.dot(vt, p, preferred_element_type=F32)
            out.append((m_new, l_new, acc))
        return tuple(out)

    init = tuple((jnp.full((1, tq), NEG, F32), jnp.zeros((1, tq), F32), jnp.zeros((HEAD_DIM, tq), F32))
                 for _ in range(nh))
    state = lax.fori_loop(0, qi, lambda j, st: update(j, st, 0, tq, 0, False), init)
    final = update(qi, state, 0, tq, 0, True)
    out_t = jnp.concatenate([acc / l for _, l, acc in final], axis=0)
    o_ref[...] = out_t.T.astype(BF16)


def _fox_attention(q_aug, k_aug, v_t):
    batch, nt, _, tq = v_t.shape
    assert tq == FOX_TILE
    seq = nt * tq
    nh = FOX_HEADS
    q3 = q_aug.reshape(batch, seq, AUG_WIDTH)
    return pl.pallas_call(
        functools.partial(_fox_kernel, tq=tq, nh=nh),
        grid=(batch, B_HEADS // nh, nt),
        in_specs=[pl.BlockSpec((None, tq, nh * LANES), lambda b, hp, i: (b, i, hp)),
                  pl.BlockSpec((None, seq, nh * LANES), lambda b, hp, i: (b, 0, hp)),
                  pl.BlockSpec((None, nt, nh * HEAD_DIM, tq), lambda b, hp, i: (b, 0, hp, 0))],
        out_specs=pl.BlockSpec((None, tq, nh * HEAD_DIM), lambda b, hp, i: (b, i, hp)),
        out_shape=jax.ShapeDtypeStruct((batch, seq, B_WIDTH), BF16),
        compiler_params=_params(3),
        name="fox_attn",
    )(q3, k_aug, v_t)


def _bias_placement():
    assert 2 * BIAS_PIECES <= BIAS_STRIDE
    place_q = np.zeros((BIAS_PIECES * LANES, LANES), np.float32)
    place_k = np.zeros((BIAS_PIECES * LANES, LANES), np.float32)
    ones_q = np.zeros((1, LANES), np.float32)
    ones_k = np.zeros((1, LANES), np.float32)
    for h in range(B_HEADS):
        base = h * BIAS_STRIDE
        for piece in range(BIAS_PIECES):
            place_q[piece * LANES + h, base + piece] = 1.0
            ones_k[0, base + piece] = 1.0
            place_k[piece * LANES + h, base + BIAS_PIECES + piece] = -1.0
            ones_q[0, base + BIAS_PIECES + piece] = 1.0
    return (jnp.asarray(place_q, BF16), jnp.asarray(ones_q), jnp.asarray(place_k, BF16), jnp.asarray(ones_k))


def _rope_lane_tables(seq):
    pos = jnp.arange(seq, dtype=F32)
    inv = ROPE_THETA ** (-jnp.arange(0, ROPE_DIM, 2, dtype=F32) / ROPE_DIM)
    ang = pos[:, None] * inv[None, :]
    cos, sin = jnp.cos(ang), jnp.sin(ang)
    ones = jnp.ones((seq, HEAD_DIM - ROPE_DIM), F32)
    zeros = jnp.zeros((seq, HEAD_DIM - ROPE_DIM), F32)
    zh = jnp.zeros((seq, ROPE_HALF), F32)
    cos_h = jnp.concatenate([cos, cos, ones], axis=1)
    sa_h = jnp.concatenate([zh, sin, zeros], axis=1)
    sb_h = jnp.concatenate([-sin, zh, zeros], axis=1)
    rep = LANES // HEAD_DIM
    return tuple(jnp.tile(t, (1, rep)) for t in (cos_h, sa_h, sb_h))


def _group_major(w_qkv):
    d = w_qkv.shape[0]
    w = w_qkv.reshape(d, 3, len(A_GROUPS), GROUP_WIDTH).transpose(0, 2, 1, 3)
    return w.reshape(d, 3 * A_WIDTH)


def kernel(x, norm_gains, w_qkv_a, w_o_a, w_q_b, w_o_b, kv_norm, w_kvf, b_f, w_up, conv_w, conv_b, w_down):
    batch, seq, _ = x.shape
    m = batch * seq
    cos_t, sa_t, sb_t = _rope_lane_tables(seq)
    place_q, ones_q, place_k, ones_k = _bias_placement()
    gains = norm_gains.reshape(DEPTH * 4, 1, D_MODEL)
    w_qkv_a, w_o_a, w_q_b, w_o_b, w_up, w_down = (
        t.astype(BF16) for t in (w_qkv_a, w_o_a, w_q_b, w_o_b, w_up, w_down))
    w_kv = w_kvf[:, :2 * B_WIDTH].astype(BF16)
    w_f = jnp.pad(w_kvf[:, 2 * B_WIDTH:], ((0, 0), (0, LANES - B_HEADS))).astype(BF16)
    b_f = jnp.pad(b_f, (0, LANES - B_HEADS)).reshape(1, LANES)
    conv_b = conv_b.reshape(DEPTH, 1, 2 * D_FF)

    h = x
    k_aug = v_sh = c_sh = None
    for l in range(DEPTH):
        g = lambda i: gains[4 * l + i]
        if l < N_A:
            qkv = _qkv_proj(h, g(0), _group_major(w_qkv_a[l]), cos_t, sa_t, sb_t)
            os_, ls_ = zip(*[_dilated_attention(qkv[gi], gi, window, r)
                             for gi, (window, r) in enumerate(A_GROUPS)])
            h = _oproj_a(os_, ls_, w_o_a[l], h, g(1))
        else:
            if l == N_A:
                k_aug, v_sh, c_sh = _kvf_proj(h, kv_norm.reshape(1, D_MODEL), w_kv, w_f, b_f,
                                              place_k, ones_k)
            j = l - N_A
            q_aug = _q_proj(h.reshape(m, D_MODEL), g(0), w_q_b[j], c_sh.reshape(m, LANES),
                            place_q, ones_q)
            o = _fox_attention(q_aug, k_aug, v_sh)
            h = _oproj_b(o.reshape(m, B_WIDTH), w_o_b[j], h.reshape(m, D_MODEL), g(1))
            h = h.reshape(batch, seq, D_MODEL)
        h = _conv_ffn(h, l, gains, w_up, conv_w, conv_b, w_down)
    return h
```

```python
import functools

import numpy as np

import jax
import jax.numpy as jnp
from jax import lax
from jax.experimental import pallas as pl
from jax.experimental.pallas import tpu as pltpu

D_MODEL = 1024
DEPTH = 4
HEAD_DIM = 64
A_GROUPS = ((128, 1), (512, 4), (2048, 16))
A_HEADS_PER_GROUP = 4
A_HEADS = A_HEADS_PER_GROUP * len(A_GROUPS)
A_WIDTH = A_HEADS * HEAD_DIM
GROUP_WIDTH = A_HEADS_PER_GROUP * HEAD_DIM
B_HEADS = D_MODEL // HEAD_DIM
B_WIDTH = B_HEADS * HEAD_DIM
N_A = DEPTH // 2
D_FF = 2816
CONV_W = 3
ROPE_DIM = HEAD_DIM // 4
ROPE_HALF = ROPE_DIM // 2
ROPE_THETA = 500000.0
BLK = 128
EPS = 1e-6
NEG = -1e30
SCALE = HEAD_DIM ** -0.5
LOG2E = float(np.log2(np.e))

LANES = 128
SUBLANES = 8
MXU_DIM = 256
VMEM_LIMIT = 56 * 1024 * 1024

ROW_TILE = 512
FFN_CHUNK = MXU_DIM
FFN_GROUPS = (4, 4, 3)
FOX_TILE = 512
VT_ROWS = HEAD_DIM + 16
FOX_HEADS = 4
AUG_WIDTH = B_HEADS * LANES
BIAS_PIECES = 3
BIAS_STRIDE = LANES // B_HEADS
DILATED_PLAN = ((1, 3), (4, 4), (16, 4))

F32 = jnp.float32
BF16 = jnp.bfloat16
NT_DIMS = (((1,), (1,)), ((), ()))


def _params(n_axes):
    return pltpu.CompilerParams(dimension_semantics=("arbitrary",) * n_axes,
                                vmem_limit_bytes=VMEM_LIMIT)


def _resident(shape, index_map):
    return pl.BlockSpec(shape, index_map, pipeline_mode=pl.Buffered(1))


def _rms(x, g):
    return x * lax.rsqrt(jnp.mean(x * x, axis=-1, keepdims=True) + EPS) * g


def _qkv_kernel(x_ref, g_ref, w_ref, cos_ref, sa_ref, sb_ref, o0_ref, o1_ref, o2_ref, ybuf_ref, *, tm):
    xn = _rms(x_ref[...], g_ref[...]).astype(BF16)
    y = jnp.dot(xn, w_ref[...], preferred_element_type=F32)
    cos, sa, sb = cos_ref[...], sa_ref[...], sb_ref[...]
    slabs_per_group = 3 * GROUP_WIDTH // LANES
    slabs_per_kind = GROUP_WIDTH // LANES
    for c in range(3 * A_WIDTH // LANES):
        grp, kind = c // slabs_per_group, (c % slabs_per_group) // slabs_per_kind
        t = y[:, c * LANES:(c + 1) * LANES]
        if kind < 2:
            t = t * cos + pltpu.roll(t, ROPE_HALF, 1) * sa + pltpu.roll(t, LANES - ROPE_HALF, 1) * sb
        if kind == 0:
            t = t * SCALE
        if grp == 0:
            o0_ref[0, :, c * LANES:(c + 1) * LANES] = t.astype(BF16)
        else:
            ybuf_ref[c - slabs_per_group] = t
    for grp, o_ref in ((1, o1_ref), (2, o2_ref)):
        r = A_GROUPS[grp][1]
        for j in range(r):
            for c in range(slabs_per_group):
                o_ref[j, :, c * LANES:(c + 1) * LANES] = ybuf_ref[
                    (grp - 1) * slabs_per_group + c, pl.ds(j, tm // r, stride=r), :].astype(BF16)


def _qkv_proj(h3, g, w, cos_t, sa_t, sb_t):
    batch, seq, _ = h3.shape
    tm = ROW_TILE
    n = 3 * A_WIDTH
    gw = 3 * GROUP_WIDTH
    tab = pl.BlockSpec((tm, LANES), lambda b, t: (t, 0))
    (_, r1), (_, r2) = A_GROUPS[1], A_GROUPS[2]
    return pl.pallas_call(
        functools.partial(_qkv_kernel, tm=tm),
        grid=(batch, seq // tm),
        in_specs=[pl.BlockSpec((None, tm, D_MODEL), lambda b, t: (b, t, 0)),
                  _resident((1, D_MODEL), lambda b, t: (0, 0)),
                  _resident((D_MODEL, n), lambda b, t: (0, 0)),
                  tab, tab, tab],
        out_specs=[pl.BlockSpec((None, 1, tm, gw), lambda b, t: (b, 0, t, 0)),
                   pl.BlockSpec((None, r1, tm // r1, gw), lambda b, t: (b, 0, t, 0)),
                   pl.BlockSpec((None, r2, tm // r2, gw), lambda b, t: (b, 0, t, 0))],
        out_shape=[jax.ShapeDtypeStruct((batch, 1, seq, gw), BF16),
                   jax.ShapeDtypeStruct((batch, r1, seq // r1, gw), BF16),
                   jax.ShapeDtypeStruct((batch, r2, seq // r2, gw), BF16)],
        scratch_shapes=[pltpu.VMEM((2 * gw // LANES, tm, LANES), F32)],
        compiler_params=_params(2),
        name="qkv_proj",
    )(h3, g, w, cos_t, sa_t, sb_t)


def _dilated_kernel(q_ref, k_ref, v_ref, o_ref, lse_ref, *, n_res, nb, n_back, unroll):
    nh = A_HEADS_PER_GROUP
    lane = lax.broadcasted_iota(jnp.int32, (BLK, GROUP_WIDTH), 1)
    head_masks = [(lane >= h * HEAD_DIM) & (lane < (h + 1) * HEAD_DIM) for h in range(nh)]

    def band(width):
        qi = lax.broadcasted_iota(jnp.int32, (nh * BLK, width), 0) & (BLK - 1)
        kj = lax.broadcasted_iota(jnp.int32, (nh * BLK, width), 1)
        rel = qi + (width - BLK) - kj
        return (rel >= 0) & (rel <= n_back)

    def attend(items, width):
        mask = band(width)
        scores = []
        for res, q0, k0 in items:
            q = q_ref[res, pl.ds(q0, BLK), :]
            qs = jnp.concatenate([jnp.where(hm, q, jnp.zeros_like(q)) for hm in head_masks], axis=0)
            scores.append(lax.dot_general(qs, k_ref[res, pl.ds(k0, width), :], NT_DIMS,
                                          preferred_element_type=F32))
        probs = []
        for s in scores:
            s = jnp.where(mask, s, NEG)
            m = jnp.max(s, axis=-1, keepdims=True)
            p = jnp.exp(s - m)
            l = jnp.sum(p, axis=-1, keepdims=True)
            probs.append(((p / l).astype(BF16), m + jnp.log(l)))
        for (res, q0, k0), (pn, lse) in zip(items, probs):
            pv = jnp.dot(pn, v_ref[res, pl.ds(k0, width), :], preferred_element_type=F32)
            o = pv[0:BLK]
            ls = jnp.broadcast_to(lse[0:BLK], (BLK, GROUP_WIDTH))
            for h in range(1, nh):
                o = jnp.where(head_masks[h], pv[h * BLK:(h + 1) * BLK], o)
                ls = jnp.where(head_masks[h], lse[h * BLK:(h + 1) * BLK], ls)
            o_ref[res, pl.ds(q0, BLK), :] = o.astype(BF16)
            lse_ref[res, pl.ds(q0, BLK), :] = ls

    if n_res <= unroll:
        attend([(res, 0, 0) for res in range(n_res)], BLK)
    else:
        def first(it, carry):
            attend([(it * unroll + u, 0, 0) for u in range(unroll)], BLK)
            return carry
        lax.fori_loop(0, n_res // unroll, first, 0)

    if nb > 1 and n_res == 1:
        def rest(it, carry):
            items = []
            for u in range(unroll):
                q0 = pl.multiple_of((1 + it * unroll + u) * BLK, BLK)
                items.append((0, q0, pl.multiple_of(q0 - BLK, BLK)))
            attend(items, 2 * BLK)
            return carry
        lax.fori_loop(0, (nb - 1) // unroll, rest, 0)
    elif nb > 1:
        def rest(i, carry):
            q0 = pl.multiple_of(i * BLK, BLK)
            attend([(res, q0, pl.multiple_of(q0 - BLK, BLK)) for res in range(n_res)], 2 * BLK)
            return carry
        lax.fori_loop(1, nb, rest, 0)


def _dilated_attention(qkv_g, g, window, r):
    batch, _, ln, _ = qkv_g.shape
    n_res, unroll = DILATED_PLAN[g]
    nb = ln // BLK
    assert ln % BLK == 0 and r % n_res == 0
    assert n_res % unroll == 0 or n_res < unroll
    assert n_res > 1 or (nb - 1) % unroll == 0
    assert n_res == 1 or n_res <= unroll or nb == 1

    def in_spec(which):
        return pl.BlockSpec((None, n_res, ln, GROUP_WIDTH), lambda b, j: (b, j, 0, which))

    out_spec = pl.BlockSpec((None, n_res, ln, GROUP_WIDTH), lambda b, j: (b, j, 0, 0))
    return pl.pallas_call(
        functools.partial(_dilated_kernel, n_res=n_res, nb=nb, n_back=window // r, unroll=unroll),
        grid=(batch, r // n_res),
        in_specs=[in_spec(0), in_spec(1), in_spec(2)],
        out_specs=[out_spec, out_spec],
        out_shape=[jax.ShapeDtypeStruct((batch, r, ln, GROUP_WIDTH), BF16),
                   jax.ShapeDtypeStruct((batch, r, ln, GROUP_WIDTH), F32)],
        compiler_params=_params(2),
        name=f"dilated_attn_g{g}",
    )(qkv_g, qkv_g, qkv_g)


def _mix_dilated(o0_ref, o1_ref, o2_ref, l0_ref, l1_ref, l2_ref, w_ref, obuf_ref, lbuf_ref, *, tm):
    slabs = GROUP_WIDTH // LANES
    for grp, (o_ref, l_ref) in enumerate(((o0_ref, l0_ref), (o1_ref, l1_ref), (o2_ref, l2_ref))):
        r = A_GROUPS[grp][1]
        for j in range(r):
            rows = pl.ds(j, tm // r, stride=r) if r > 1 else slice(None)
            for c in range(slabs):
                cols = slice(c * LANES, (c + 1) * LANES)
                obuf_ref[grp * slabs + c, rows, :] = o_ref[j, :, cols].astype(F32)
                lbuf_ref[grp * slabs + c, rows, :] = l_ref[j, :, cols]

    def group(buf_ref, grp):
        return jnp.concatenate([buf_ref[grp * slabs + c] for c in range(slabs)], axis=1)

    l0, l1, l2 = (group(lbuf_ref, grp) for grp in range(3))
    mx = jnp.maximum(jnp.maximum(l0, l1), l2)
    e0, e1, e2 = jnp.exp(l0 - mx), jnp.exp(l1 - mx), jnp.exp(l2 - mx)
    den = e0 + e1 + e2
    mixed = jnp.concatenate([(group(obuf_ref, grp) * (e / den)).astype(BF16)
                             for grp, e in enumerate((e0, e1, e2))], axis=1)
    return jnp.dot(mixed, w_ref[...], preferred_element_type=F32)


def _tail_kernel(*refs, tm, n_mix):
    mix_refs, refs = refs[:n_mix + 1], refs[n_mix + 1:]
    h_ref, g1_ref, g2_ref, wup_ref, cw_ref, cb_ref, wdn_ref, g3_ref, out_ref, carry_ref = refs[:10]
    scratch = refs[10:]
    n_chunks = D_FF // FFN_CHUNK
    abufs = scratch[:2 * n_chunks]
    hmids = scratch[2 * n_chunks:2 * n_chunks + len(FFN_GROUPS)]
    extra = scratch[2 * n_chunks + len(FFN_GROUPS):]

    @pl.when(pl.program_id(1) == 0)
    def _():
        carry_ref[...] = jnp.zeros_like(carry_ref)

    if n_mix == 1:
        o_ref, w_ref = mix_refs
        mix = jnp.dot(o_ref[...], w_ref[...], preferred_element_type=F32)
    else:
        mix = _mix_dilated(*mix_refs, *extra, tm=tm)
    h = h_ref[...] + _rms(mix, g1_ref[...])
    xn = _rms(h, g2_ref[...]).astype(BF16)

    def conv(col, abuf_ref):
        cs = slice(col, col + FFN_CHUNK)
        a = jnp.dot(xn, wup_ref[:, cs], preferred_element_type=F32)
        abuf_ref[0:SUBLANES, :] = carry_ref[:, cs]
        abuf_ref[SUBLANES:SUBLANES + tm, :] = a
        carry_ref[:, cs] = a[tm - SUBLANES:tm, :]
        a1 = abuf_ref[SUBLANES - 1:SUBLANES - 1 + tm, :]
        a2 = abuf_ref[SUBLANES - 2:SUBLANES - 2 + tm, :]
        return (a2 * cw_ref[0:1, cs] + a1 * cw_ref[1:2, cs] + a * cw_ref[2:3, cs]) + cb_ref[:, cs]

    f = None
    first = 0
    for group, hmid_ref in zip(FFN_GROUPS, hmids):
        for i in range(group):
            c = first + i
            gate = conv(c * FFN_CHUNK, abufs[2 * c])
            val = conv(D_FF + c * FFN_CHUNK, abufs[2 * c + 1])
            hmid_ref[:, i * FFN_CHUNK:(i + 1) * FFN_CHUNK] = (
                jax.nn.gelu(gate, approximate=True) * val).astype(BF16)
        ks = slice(first * FFN_CHUNK, (first + group) * FFN_CHUNK)
        part = jnp.dot(hmid_ref[...], wdn_ref[ks, :], preferred_element_type=F32)
        f = part if f is None else f + part
        first += group
    out_ref[...] = h + _rms(f, g3_ref[...])


def _layer_tail(mix_inputs, w_o, h3, layer, g_all, w_up, cw, cb, w_down):
    batch, seq, _ = h3.shape
    tm = ROW_TILE
    row = pl.BlockSpec((None, tm, D_MODEL), lambda b, t: (b, t, 0))
    gain = lambda i: _resident((None, 1, D_MODEL), lambda b, t: (4 * layer + i, 0, 0))
    if len(mix_inputs) == 1:
        mix_specs = [pl.BlockSpec((None, tm, B_WIDTH), lambda b, t: (b, t, 0))]
        extra_scratch = []
    else:
        mix_specs = 2 * [pl.BlockSpec((None, r, tm // r, GROUP_WIDTH), lambda b, t: (b, 0, t, 0))
                         for _, r in A_GROUPS]
        extra_scratch = 2 * [pltpu.VMEM((A_WIDTH // LANES, tm, LANES), F32)]
    return pl.pallas_call(
        functools.partial(_tail_kernel, tm=tm, n_mix=len(mix_inputs)),
        grid=(batch, seq // tm),
        in_specs=mix_specs + [_resident(w_o.shape, lambda b, t: (0, 0)),
                              row, gain(1), gain(2),
                              _resident((None, D_MODEL, 2 * D_FF), lambda b, t: (layer, 0, 0)),
                              _resident((None, CONV_W, 2 * D_FF), lambda b, t: (layer, 0, 0)),
                              _resident((None, 1, 2 * D_FF), lambda b, t: (layer, 0, 0)),
                              _resident((None, D_FF, D_MODEL), lambda b, t: (layer, 0, 0)),
                              gain(3)],
        out_specs=row,
        out_shape=jax.ShapeDtypeStruct(h3.shape, F32),
        scratch_shapes=([pltpu.VMEM((SUBLANES, 2 * D_FF), F32)]
                        + [pltpu.VMEM((tm + SUBLANES, FFN_CHUNK), F32)] * (2 * D_FF // FFN_CHUNK)
                        + [pltpu.VMEM((tm, group * FFN_CHUNK), BF16) for group in FFN_GROUPS]
                        + extra_scratch),
        compiler_params=_params(2),
        name="layer_tail",
    )(*mix_inputs, w_o, h3, g_all, g_all, w_up, cw, cb, w_down, g_all)


def _split_bf16(c):
    pieces, rest = [], c
    for _ in range(BIAS_PIECES):
        piece = rest.astype(BF16)
        pieces.append(piece)
        rest = rest - piece.astype(F32)
    return jnp.concatenate(pieces, axis=1)


def _store_augmented(x, c_tile, place_ref, ones_ref, out_ref):
    tm = x.shape[0]
    slab = jnp.dot(_split_bf16(c_tile * LOG2E), place_ref[...], preferred_element_type=F32) + ones_ref[...]
    lane = lax.broadcasted_iota(jnp.int32, (tm, LANES), 1)
    own = lane < HEAD_DIM
    bias_lanes = lane < HEAD_DIM + 2 * BIAS_PIECES
    for h in range(B_HEADS):
        pair = x[:, (h // 2) * LANES:(h // 2 + 1) * LANES]
        if h % 2 == 1:
            pair = pltpu.roll(pair, HEAD_DIM, 1)
        shift = (HEAD_DIM - BIAS_STRIDE * h) % LANES
        bias = pltpu.roll(slab, shift, 1) if shift else slab
        out_ref[:, h * LANES:(h + 1) * LANES] = jnp.where(
            own, pair, jnp.where(bias_lanes, bias, 0.0)).astype(BF16)


def _kvf_kernel(x_ref, g_ref, wkv_ref, wf_ref, bf_ref, place_ref, ones_ref, k_ref, v_ref, c_ref,
                carry_ref, *, tm):
    @pl.when(pl.program_id(1) == 0)
    def _():
        carry_ref[...] = jnp.zeros_like(carry_ref)

    xn = _rms(x_ref[...], g_ref[...]).astype(BF16)
    kv = jnp.dot(xn, wkv_ref[...], preferred_element_type=F32)
    vt = kv[:, B_WIDTH:].T
    pad_rows = lax.broadcasted_iota(jnp.int32, (VT_ROWS - HEAD_DIM, tm), 0)
    ones_pad = jnp.where(pad_rows == 0, 1.0, 0.0)
    for h in range(B_HEADS):
        v_ref[h * VT_ROWS:(h + 1) * VT_ROWS, :] = jnp.concatenate(
            [vt[h * HEAD_DIM:(h + 1) * HEAD_DIM], ones_pad], axis=0).astype(BF16)
    z = jnp.dot(xn, wf_ref[...], preferred_element_type=F32) + bf_ref[...]
    log_f = -(jnp.maximum(-z, 0.0) + jnp.log1p(jnp.exp(-jnp.abs(z))))
    rows = lax.broadcasted_iota(jnp.int32, (tm, LANES), 0)
    acc = log_f
    shift = 1
    while shift < tm:
        acc = acc + jnp.where(rows >= shift, pltpu.roll(acc, shift, 0), 0.0)
        shift *= 2
    acc = acc + carry_ref[...]
    c_ref[...] = acc
    carry_ref[...] = acc[tm - 1:tm, :]
    _store_augmented(kv[:, :B_WIDTH], acc, place_ref, ones_ref, k_ref)


def _kvf_proj(h3, g, wkv, wf, bf, place_k, ones_k):
    batch, seq, _ = h3.shape
    tm = ROW_TILE
    row = lambda width: pl.BlockSpec((None, tm, width), lambda b, t: (b, t, 0))
    return pl.pallas_call(
        functools.partial(_kvf_kernel, tm=tm),
        grid=(batch, seq // tm),
        in_specs=[row(D_MODEL),
                  _resident((1, D_MODEL), lambda b, t: (0, 0)),
                  _resident((D_MODEL, 2 * B_WIDTH), lambda b, t: (0, 0)),
                  _resident((D_MODEL, LANES), lambda b, t: (0, 0)),
                  _resident((1, LANES), lambda b, t: (0, 0)),
                  _resident((BIAS_PIECES * LANES, LANES), lambda b, t: (0, 0)),
                  _resident((1, LANES), lambda b, t: (0, 0))],
        out_specs=[row(AUG_WIDTH),
                   pl.BlockSpec((None, None, B_HEADS * VT_ROWS, tm), lambda b, t: (b, t, 0, 0)),
                   row(LANES)],
        out_shape=[jax.ShapeDtypeStruct((batch, seq, AUG_WIDTH), BF16),
                   jax.ShapeDtypeStruct((batch, seq // tm, B_HEADS * VT_ROWS, tm), BF16),
                   jax.ShapeDtypeStruct((batch, seq, LANES), F32)],
        scratch_shapes=[pltpu.VMEM((1, LANES), F32)],
        compiler_params=_params(2),
        name="kvf_proj",
    )(h3, g, wkv, wf, bf, place_k, ones_k)


def _qproj_kernel(x_ref, g_ref, w_ref, c_ref, place_ref, ones_ref, o_ref):
    xn = _rms(x_ref[...], g_ref[...]).astype(BF16)
    q = jnp.dot(xn, w_ref[...], preferred_element_type=F32) * (SCALE * LOG2E)
    _store_augmented(q, c_ref[...], place_ref, ones_ref, o_ref)


def _q_proj(h2, g, w, c2, place_q, ones_q):
    m = h2.shape[0]
    tm = ROW_TILE
    return pl.pallas_call(
        _qproj_kernel,
        grid=(m // tm,),
        in_specs=[pl.BlockSpec((tm, D_MODEL), lambda i: (i, 0)),
                  _resident((1, D_MODEL), lambda i: (0, 0)),
                  _resident((D_MODEL, B_WIDTH), lambda i: (0, 0)),
                  pl.BlockSpec((tm, LANES), lambda i: (i, 0)),
                  _resident((BIAS_PIECES * LANES, LANES), lambda i: (0, 0)),
                  _resident((1, LANES), lambda i: (0, 0))],
        out_specs=pl.BlockSpec((tm, AUG_WIDTH), lambda i: (i, 0)),
        out_shape=jax.ShapeDtypeStruct((m, AUG_WIDTH), BF16),
        compiler_params=_params(1),
        name="q_proj",
    )(h2, g, w, c2, place_q, ones_q)


def _fox_kernel(q_ref, k_ref, vt_ref, o_ref, *, tq, nh):
    visible = (lax.broadcasted_iota(jnp.int32, (tq, tq), 0)
               <= lax.broadcasted_iota(jnp.int32, (tq, tq), 1))

    def update(qi, j, state, diagonal):
        scores = []
        for hh in range(nh):
            cols = slice(hh * LANES, (hh + 1) * LANES)
            scores.append(lax.dot_general(k_ref[j * tq:(j + 1) * tq, cols], q_ref[qi * tq:(qi + 1) * tq, cols],
                                          NT_DIMS, preferred_element_type=F32))
        stats = []
        for hh in range(nh):
            m, _ = state[hh]
            s = scores[hh]
            if diagonal:
                s = jnp.where(visible, s, NEG)
            m_new = jnp.maximum(m, jnp.max(s, axis=0, keepdims=True))
            stats.append((m_new, jnp.exp2(m - m_new), jnp.exp2(s - m_new).astype(BF16)))
        out = []
        for hh in range(nh):
            m_new, alpha, p = stats[hh]
            vt = vt_ref[j, hh * VT_ROWS:(hh + 1) * VT_ROWS, :]
            out.append((m_new, alpha * state[hh][1] + jnp.dot(vt, p, preferred_element_type=F32)))
        return tuple(out)

    for qi in range(q_ref.shape[0] // tq):
        state = tuple((jnp.full((1, tq), NEG, F32), jnp.zeros((VT_ROWS, tq), F32)) for _ in range(nh))
        for j in range(qi):
            state = update(qi, j, state, False)
        final = update(qi, qi, state, True)
        out_t = jnp.concatenate([acc[:HEAD_DIM] / acc[HEAD_DIM:HEAD_DIM + 1] for _, acc in final], axis=0)
        o_ref[qi * tq:(qi + 1) * tq, :] = out_t.T.astype(BF16)


def _fox_attention(q_aug, k_aug, v_t):
    batch, nt, _, tq = v_t.shape
    assert tq == FOX_TILE
    seq = nt * tq
    nh = FOX_HEADS
    q3 = q_aug.reshape(batch, seq, AUG_WIDTH)
    return pl.pallas_call(
        functools.partial(_fox_kernel, tq=tq, nh=nh),
        grid=(batch, B_HEADS // nh),
        in_specs=[pl.BlockSpec((None, seq, nh * LANES), lambda b, hp: (b, 0, hp)),
                  pl.BlockSpec((None, seq, nh * LANES), lambda b, hp: (b, 0, hp)),
                  pl.BlockSpec((None, nt, nh * VT_ROWS, tq), lambda b, hp: (b, 0, hp, 0))],
        out_specs=pl.BlockSpec((None, seq, nh * HEAD_DIM), lambda b, hp: (b, 0, hp)),
        out_shape=jax.ShapeDtypeStruct((batch, seq, B_WIDTH), BF16),
        compiler_params=_params(2),
        name="fox_attn",
    )(q3, k_aug, v_t)


def _bias_placement():
    assert 2 * BIAS_PIECES <= BIAS_STRIDE
    place_q = np.zeros((BIAS_PIECES * LANES, LANES), np.float32)
    place_k = np.zeros((BIAS_PIECES * LANES, LANES), np.float32)
    ones_q = np.zeros((1, LANES), np.float32)
    ones_k = np.zeros((1, LANES), np.float32)
    for h in range(B_HEADS):
        base = h * BIAS_STRIDE
        for piece in range(BIAS_PIECES):
            place_q[piece * LANES + h, base + piece] = 1.0
            ones_k[0, base + piece] = 1.0
            place_k[piece * LANES + h, base + BIAS_PIECES + piece] = -1.0
            ones_q[0, base + BIAS_PIECES + piece] = 1.0
    return (jnp.asarray(place_q, BF16), jnp.asarray(ones_q), jnp.asarray(place_k, BF16), jnp.asarray(ones_k))


def _rope_lane_tables(seq):
    pos = jnp.arange(seq, dtype=F32)
    inv = ROPE_THETA ** (-jnp.arange(0, ROPE_DIM, 2, dtype=F32) / ROPE_DIM)
    ang = pos[:, None] * inv[None, :]
    cos, sin = jnp.cos(ang), jnp.sin(ang)
    ones = jnp.ones((seq, HEAD_DIM - ROPE_DIM), F32)
    zeros = jnp.zeros((seq, HEAD_DIM - ROPE_DIM), F32)
    zh = jnp.zeros((seq, ROPE_HALF), F32)
    cos_h = jnp.concatenate([cos, cos, ones], axis=1)
    sa_h = jnp.concatenate([zh, sin, zeros], axis=1)
    sb_h = jnp.concatenate([-sin, zh, zeros], axis=1)
    rep = LANES // HEAD_DIM
    return tuple(jnp.tile(t, (1, rep)) for t in (cos_h, sa_h, sb_h))


def _group_major(w_qkv):
    d = w_qkv.shape[0]
    w = w_qkv.reshape(d, 3, len(A_GROUPS), GROUP_WIDTH).transpose(0, 2, 1, 3)
    return w.reshape(d, 3 * A_WIDTH)


def kernel(x, norm_gains, w_qkv_a, w_o_a, w_q_b, w_o_b, kv_norm, w_kvf, b_f, w_up, conv_w, conv_b, w_down):
    batch, seq, _ = x.shape
    m = batch * seq
    cos_t, sa_t, sb_t = _rope_lane_tables(seq)
    place_q, ones_q, place_k, ones_k = _bias_placement()
    gains = norm_gains.reshape(DEPTH * 4, 1, D_MODEL)
    w_qkv_a, w_o_a, w_q_b, w_o_b, w_up, w_down = (
        t.astype(BF16) for t in (w_qkv_a, w_o_a, w_q_b, w_o_b, w_up, w_down))
    w_kv = w_kvf[:, :2 * B_WIDTH].astype(BF16)
    w_f = jnp.pad(w_kvf[:, 2 * B_WIDTH:], ((0, 0), (0, LANES - B_HEADS))).astype(BF16)
    b_f = jnp.pad(b_f, (0, LANES - B_HEADS)).reshape(1, LANES)
    conv_b = conv_b.reshape(DEPTH, 1, 2 * D_FF)

    h = x
    k_aug = v_sh = c_sh = None
    for l in range(DEPTH):
        g = lambda i: gains[4 * l + i]
        if l < N_A:
            qkv = _qkv_proj(h, g(0), _group_major(w_qkv_a[l]), cos_t, sa_t, sb_t)
            os_, ls_ = zip(*[_dilated_attention(qkv[gi], gi, window, r)
                             for gi, (window, r) in enumerate(A_GROUPS)])
            mix_inputs, w_o = [*os_, *ls_], w_o_a[l]
        else:
            if l == N_A:
                k_aug, v_sh, c_sh = _kvf_proj(h, kv_norm.reshape(1, D_MODEL), w_kv, w_f, b_f,
                                              place_k, ones_k)
            j = l - N_A
            q_aug = _q_proj(h.reshape(m, D_MODEL), g(0), w_q_b[j], c_sh.reshape(m, LANES),
                            place_q, ones_q)
            mix_inputs, w_o = [_fox_attention(q_aug, k_aug, v_sh)], w_o_b[j]
        h = _layer_tail(mix_inputs, w_o, h, l, gains, w_up, conv_w, conv_b, w_down)
    return h
```

```python
import functools

import numpy as np

import jax
import jax.numpy as jnp
from jax import lax
from jax.experimental import pallas as pl
from jax.experimental.pallas import tpu as pltpu

D_MODEL = 1024
DEPTH = 4
HEAD_DIM = 64
A_GROUPS = ((128, 1), (512, 4), (2048, 16))
A_HEADS_PER_GROUP = 4
A_HEADS = A_HEADS_PER_GROUP * len(A_GROUPS)
A_WIDTH = A_HEADS * HEAD_DIM
GROUP_WIDTH = A_HEADS_PER_GROUP * HEAD_DIM
B_HEADS = D_MODEL // HEAD_DIM
B_WIDTH = B_HEADS * HEAD_DIM
N_A = DEPTH // 2
D_FF = 2816
CONV_W = 3
ROPE_DIM = HEAD_DIM // 4
ROPE_HALF = ROPE_DIM // 2
ROPE_THETA = 500000.0
BLK = 128
EPS = 1e-6
NEG = -1e30
SCALE = HEAD_DIM ** -0.5
LOG2E = float(np.log2(np.e))

LANES = 128
SUBLANES = 8
MXU_DIM = 256
VMEM_LIMIT = 56 * 1024 * 1024

ROW_TILE = 512
FFN_CHUNK = MXU_DIM
STRIDE_PITCH = ROW_TILE // SUBLANES + SUBLANES
TAIL_SUBTILES = 1
FFN_GROUPS = (4, 4, 3)
FOX_TILE = 512
VT_ROWS = HEAD_DIM + 16
FOX_HEADS = 4
AUG_WIDTH = B_HEADS * LANES
BIAS_PIECES = 3
BIAS_STRIDE = LANES // B_HEADS
DILATED_PLAN = ((1, 3), (4, 4), (16, 4))

F32 = jnp.float32
BF16 = jnp.bfloat16
NT_DIMS = (((1,), (1,)), ((), ()))


def _params(n_axes):
    return pltpu.CompilerParams(dimension_semantics=("arbitrary",) * n_axes,
                                vmem_limit_bytes=VMEM_LIMIT)


def _resident(shape, index_map):
    return pl.BlockSpec(shape, index_map, pipeline_mode=pl.Buffered(1))


def _rms(x, g):
    return x * lax.rsqrt(jnp.mean(x * x, axis=-1, keepdims=True) + EPS) * g


def _qkv_kernel(x_ref, g_ref, w_ref, cos_ref, sa_ref, sb_ref, o0_ref, o1_ref, o2_ref, ybuf_ref, *, tm):
    xn = _rms(x_ref[...], g_ref[...]).astype(BF16)
    y = jnp.dot(xn, w_ref[...], preferred_element_type=F32)
    cos, sa, sb = cos_ref[...], sa_ref[...], sb_ref[...]
    slabs_per_group = 3 * GROUP_WIDTH // LANES
    slabs_per_kind = GROUP_WIDTH // LANES
    for c in range(3 * A_WIDTH // LANES):
        grp, kind = c // slabs_per_group, (c % slabs_per_group) // slabs_per_kind
        t = y[:, c * LANES:(c + 1) * LANES]
        if kind < 2:
            t = t * cos + pltpu.roll(t, ROPE_HALF, 1) * sa + pltpu.roll(t, LANES - ROPE_HALF, 1) * sb
        if kind == 0:
            t = t * SCALE
        if grp == 0:
            o0_ref[0, :, c * LANES:(c + 1) * LANES] = t.astype(BF16)
        else:
            ybuf_ref[c - slabs_per_group] = t
    for grp, o_ref in ((1, o1_ref), (2, o2_ref)):
        r = A_GROUPS[grp][1]
        for j in range(r):
            for c in range(slabs_per_group):
                o_ref[j, :, c * LANES:(c + 1) * LANES] = ybuf_ref[
                    (grp - 1) * slabs_per_group + c, pl.ds(j, tm // r, stride=r), :].astype(BF16)


def _qkv_proj(h3, g, w, cos_t, sa_t, sb_t):
    batch, seq, _ = h3.shape
    tm = ROW_TILE
    n = 3 * A_WIDTH
    gw = 3 * GROUP_WIDTH
    tab = pl.BlockSpec((tm, LANES), lambda b, t: (t, 0))
    (_, r1), (_, r2) = A_GROUPS[1], A_GROUPS[2]
    return pl.pallas_call(
        functools.partial(_qkv_kernel, tm=tm),
        grid=(batch, seq // tm),
        in_specs=[pl.BlockSpec((None, tm, D_MODEL), lambda b, t: (b, t, 0)),
                  _resident((1, D_MODEL), lambda b, t: (0, 0)),
                  _resident((D_MODEL, n), lambda b, t: (0, 0)),
                  tab, tab, tab],
        out_specs=[pl.BlockSpec((None, 1, tm, gw), lambda b, t: (b, 0, t, 0)),
                   pl.BlockSpec((None, r1, tm // r1, gw), lambda b, t: (b, 0, t, 0)),
                   pl.BlockSpec((None, r2, tm // r2, gw), lambda b, t: (b, 0, t, 0))],
        out_shape=[jax.ShapeDtypeStruct((batch, 1, seq, gw), BF16),
                   jax.ShapeDtypeStruct((batch, r1, seq // r1, gw), BF16),
                   jax.ShapeDtypeStruct((batch, r2, seq // r2, gw), BF16)],
        scratch_shapes=[pltpu.VMEM((2 * gw // LANES, tm, LANES), F32)],
        compiler_params=_params(2),
        name="qkv_proj",
    )(h3, g, w, cos_t, sa_t, sb_t)


def _dilated_kernel(q_ref, k_ref, v_ref, o_ref, lse_ref, *, n_res, nb, n_back, unroll):
    nh = A_HEADS_PER_GROUP
    lane = lax.broadcasted_iota(jnp.int32, (BLK, GROUP_WIDTH), 1)
    head_masks = [(lane >= h * HEAD_DIM) & (lane < (h + 1) * HEAD_DIM) for h in range(nh)]

    def band(width):
        qi = lax.broadcasted_iota(jnp.int32, (nh * BLK, width), 0) & (BLK - 1)
        kj = lax.broadcasted_iota(jnp.int32, (nh * BLK, width), 1)
        rel = qi + (width - BLK) - kj
        return (rel >= 0) & (rel <= n_back)

    def attend(items, width):
        mask = band(width)
        scores = []
        for res, q0, k0 in items:
            q = q_ref[res, pl.ds(q0, BLK), :]
            qs = jnp.concatenate([jnp.where(hm, q, jnp.zeros_like(q)) for hm in head_masks], axis=0)
            scores.append(lax.dot_general(qs, k_ref[res, pl.ds(k0, width), :], NT_DIMS,
                                          preferred_element_type=F32))
        probs = []
        for s in scores:
            s = jnp.where(mask, s, NEG)
            m = jnp.max(s, axis=-1, keepdims=True)
            p = jnp.exp(s - m)
            l = jnp.sum(p, axis=-1, keepdims=True)
            probs.append(((p / l).astype(BF16), m + jnp.log(l)))
        for (res, q0, k0), (pn, lse) in zip(items, probs):
            pv = jnp.dot(pn, v_ref[res, pl.ds(k0, width), :], preferred_element_type=F32)
            o = pv[0:BLK]
            ls = jnp.broadcast_to(lse[0:BLK], (BLK, GROUP_WIDTH))
            for h in range(1, nh):
                o = jnp.where(head_masks[h], pv[h * BLK:(h + 1) * BLK], o)
                ls = jnp.where(head_masks[h], lse[h * BLK:(h + 1) * BLK], ls)
            o_ref[res, pl.ds(q0, BLK), :] = o.astype(BF16)
            lse_ref[res, pl.ds(q0, BLK), :] = ls

    firsts = [(res, 0, 0) for res in range(n_res)]
    for i in range(0, n_res, unroll):
        attend(firsts[i:i + unroll], BLK)
    rest = [(res, i * BLK, (i - 1) * BLK) for i in range(1, nb) for res in range(n_res)]
    for i in range(0, len(rest), unroll):
        attend(rest[i:i + unroll], 2 * BLK)


def _dilated_attention(qkv_g, g, window, r):
    batch, _, ln, _ = qkv_g.shape
    n_res, unroll = DILATED_PLAN[g]
    nb = ln // BLK
    assert ln % BLK == 0 and r % n_res == 0
    assert n_res % unroll == 0 or n_res < unroll
    assert n_res > 1 or (nb - 1) % unroll == 0
    assert n_res == 1 or n_res <= unroll or nb == 1

    def in_spec(which):
        return pl.BlockSpec((None, n_res, ln, GROUP_WIDTH), lambda b, j: (b, j, 0, which))

    out_spec = pl.BlockSpec((None, n_res, ln, GROUP_WIDTH), lambda b, j: (b, j, 0, 0))
    return pl.pallas_call(
        functools.partial(_dilated_kernel, n_res=n_res, nb=nb, n_back=window // r, unroll=unroll),
        grid=(batch, r // n_res),
        in_specs=[in_spec(0), in_spec(1), in_spec(2)],
        out_specs=[out_spec, out_spec],
        out_shape=[jax.ShapeDtypeStruct((batch, r, ln, GROUP_WIDTH), BF16),
                   jax.ShapeDtypeStruct((batch, r, ln, GROUP_WIDTH), F32)],
        compiler_params=_params(2),
        name=f"dilated_attn_g{g}",
    )(qkv_g, qkv_g, qkv_g)


def _mix_dilated(o0_ref, o1_ref, o2_ref, l0_ref, l1_ref, l2_ref, w_ref, obuf_ref, lbuf_ref, *, tm, sub):
    slabs = GROUP_WIDTH // LANES
    for grp, (o_ref, l_ref) in enumerate(((o0_ref, l0_ref), (o1_ref, l1_ref), (o2_ref, l2_ref))):
        r = A_GROUPS[grp][1]
        src = slice(sub * tm // r, (sub + 1) * tm // r)
        for j in range(r):
            rows = pl.ds(j, tm // r, stride=r) if r > 1 else slice(None)
            for c in range(slabs):
                cols = slice(c * LANES, (c + 1) * LANES)
                obuf_ref[grp * slabs + c, rows, :] = o_ref[j, src, cols].astype(F32)
                lbuf_ref[grp * slabs + c, rows, :] = l_ref[j, src, cols]

    def group(buf_ref, grp):
        return jnp.concatenate([buf_ref[grp * slabs + c] for c in range(slabs)], axis=1)

    l0, l1, l2 = (group(lbuf_ref, grp) for grp in range(3))
    mx = jnp.maximum(jnp.maximum(l0, l1), l2)
    e0, e1, e2 = jnp.exp(l0 - mx), jnp.exp(l1 - mx), jnp.exp(l2 - mx)
    den = e0 + e1 + e2
    mixed = jnp.concatenate([(group(obuf_ref, grp) * (e / den)).astype(BF16)
                             for grp, e in enumerate((e0, e1, e2))], axis=1)
    return jnp.dot(mixed, w_ref[...], preferred_element_type=F32)


def _to_strided_rows(x, buf_ref):
    tm, d = x.shape
    n = tm // SUBLANES
    slabs = d // LANES
    for c in range(slabs):
        for s in range(SUBLANES):
            buf_ref[c, s * STRIDE_PITCH:s * STRIDE_PITCH + n, :] = x[s * n:(s + 1) * n, c * LANES:(c + 1) * LANES]
    return jnp.concatenate(
        [jnp.concatenate([buf_ref[c, pl.ds(i, SUBLANES, stride=STRIDE_PITCH), :] for c in range(slabs)], axis=1)
         for i in range(n)], axis=0)


def _store_token_rows(y, buf_ref, out_ref, row0):
    tm, d = y.shape
    n = tm // SUBLANES
    per = n // SUBLANES
    slabs = d // LANES
    for c in range(slabs):
        buf_ref[c] = y[:, c * LANES:(c + 1) * LANES]
    for j in range(n):
        start = SUBLANES * SUBLANES * (j % per) + j // per
        out_ref[row0 + j * SUBLANES:row0 + (j + 1) * SUBLANES, :] = jnp.concatenate(
            [buf_ref[c, pl.ds(start, SUBLANES, stride=SUBLANES), :] for c in range(slabs)], axis=1)


def _tail_kernel(*refs, tm, n_mix, n_sub):
    mix_refs, refs = refs[:n_mix + 1], refs[n_mix + 1:]
    h_ref, g1_ref, g2_ref, wup_ref, cw_ref, cb_ref, wdn_ref, g3_ref, out_ref, carry_ref = refs[:10]
    scratch = refs[10:]
    per_sub = len(FFN_GROUPS) + 2

    @pl.when(pl.program_id(1) == 0)
    def _():
        carry_ref[...] = jnp.zeros_like(carry_ref)

    own = scratch[:per_sub]
    for sub in range(n_sub):
        _tail_tile(mix_refs, h_ref, g1_ref, g2_ref, wup_ref, cw_ref, cb_ref, wdn_ref, g3_ref, out_ref,
                   carry_ref, own[:len(FFN_GROUPS)], own[-2], own[-1], scratch[per_sub:], tm=tm, sub=sub)


def _tail_tile(mix_refs, h_ref, g1_ref, g2_ref, wup_ref, cw_ref, cb_ref, wdn_ref, g3_ref, out_ref,
               carry_ref, hmids, pin_ref, pout_ref, extra, *, tm, sub):
    rows = slice(sub * tm, (sub + 1) * tm)
    if len(mix_refs) == 2:
        o_ref, w_ref = mix_refs
        mix = jnp.dot(o_ref[rows, :], w_ref[...], preferred_element_type=F32)
    else:
        mix = _mix_dilated(*mix_refs, *extra, tm=tm, sub=sub)
    h = _to_strided_rows(h_ref[rows, :] + _rms(mix, g1_ref[...]), pin_ref)
    xn = _rms(h, g2_ref[...]).astype(BF16)
    last_sublane = lax.broadcasted_iota(jnp.int32, (SUBLANES, FFN_CHUNK), 0) == SUBLANES - 1

    def conv(col, k):
        cs = slice(col, col + FFN_CHUNK)
        a = jnp.dot(xn, wup_ref[:, cs], preferred_element_type=F32)
        prev = carry_ref[k]
        carry_ref[k] = a[tm - 2 * SUBLANES:tm]
        wrapped = [pltpu.roll(jnp.where(last_sublane, prev[g * SUBLANES:(g + 1) * SUBLANES],
                                        a[tm - (2 - g) * SUBLANES:tm - (1 - g) * SUBLANES]), 1, 0)
                   for g in range(2)]
        a1 = jnp.concatenate([wrapped[1], a[:tm - SUBLANES]], axis=0)
        a2 = jnp.concatenate([wrapped[0], wrapped[1], a[:tm - 2 * SUBLANES]], axis=0)
        return (a2 * cw_ref[0:1, cs] + a1 * cw_ref[1:2, cs] + a * cw_ref[2:3, cs]) + cb_ref[:, cs]

    f = None
    first = 0
    for group, hmid_ref in zip(FFN_GROUPS, hmids):
        for i in range(group):
            c = first + i
            gate = conv(c * FFN_CHUNK, 2 * c)
            val = conv(D_FF + c * FFN_CHUNK, 2 * c + 1)
            hmid_ref[:, i * FFN_CHUNK:(i + 1) * FFN_CHUNK] = (
                jax.nn.gelu(gate, approximate=True) * val).astype(BF16)
        ks = slice(first * FFN_CHUNK, (first + group) * FFN_CHUNK)
        part = jnp.dot(hmid_ref[...], wdn_ref[ks, :], preferred_element_type=F32)
        f = part if f is None else f + part
        first += group
    _store_token_rows(h + _rms(f, g3_ref[...]), pout_ref, out_ref, sub * tm)


def _layer_tail(mix_inputs, w_o, h3, layer, g_all, w_up, cw, cb, w_down):
    batch, seq, _ = h3.shape
    tm = ROW_TILE
    n_sub = TAIL_SUBTILES if len(mix_inputs) == 1 else 1
    blk = tm * n_sub
    row = pl.BlockSpec((None, blk, D_MODEL), lambda b, t: (b, t, 0))
    gain = lambda i: _resident((None, 1, D_MODEL), lambda b, t: (4 * layer + i, 0, 0))
    if len(mix_inputs) == 1:
        mix_specs = [pl.BlockSpec((None, blk, B_WIDTH), lambda b, t: (b, t, 0))]
        extra_scratch = []
    else:
        mix_specs = 2 * [pl.BlockSpec((None, r, blk // r, GROUP_WIDTH), lambda b, t: (b, 0, t, 0))
                         for _, r in A_GROUPS]
        extra_scratch = 2 * [pltpu.VMEM((A_WIDTH // LANES, tm, LANES), F32)]
    return pl.pallas_call(
        functools.partial(_tail_kernel, tm=tm, n_mix=len(mix_inputs), n_sub=n_sub),
        grid=(batch, seq // blk),
        in_specs=mix_specs + [_resident(w_o.shape, lambda b, t: (0, 0)),
                              row, gain(1), gain(2),
                              _resident((None, D_MODEL, 2 * D_FF), lambda b, t: (layer, 0, 0)),
                              _resident((None, CONV_W, 2 * D_FF), lambda b, t: (layer, 0, 0)),
                              _resident((None, 1, 2 * D_FF), lambda b, t: (layer, 0, 0)),
                              _resident((None, D_FF, D_MODEL), lambda b, t: (layer, 0, 0)),
                              gain(3)],
        out_specs=row,
        out_shape=jax.ShapeDtypeStruct(h3.shape, F32),
        scratch_shapes=([pltpu.VMEM((2 * D_FF // FFN_CHUNK, 2 * SUBLANES, FFN_CHUNK), F32)]
                        + [pltpu.VMEM((tm, group * FFN_CHUNK), BF16) for group in FFN_GROUPS]
                        + [pltpu.VMEM((D_MODEL // LANES, SUBLANES * STRIDE_PITCH, LANES), F32),
                           pltpu.VMEM((D_MODEL // LANES, tm, LANES), F32)]
                        + extra_scratch),
        compiler_params=_params(2),
        name="layer_tail",
    )(*mix_inputs, w_o, h3, g_all, g_all, w_up, cw, cb, w_down, g_all)


def _split_bf16(c):
    pieces, rest = [], c
    for _ in range(BIAS_PIECES):
        piece = rest.astype(BF16)
        pieces.append(piece)
        rest = rest - piece.astype(F32)
    return jnp.concatenate(pieces, axis=1)


def _store_augmented(x, c_tile, place_ref, ones_ref, out_ref):
    tm = x.shape[0]
    slab = jnp.dot(_split_bf16(c_tile * LOG2E), place_ref[...], preferred_element_type=F32) + ones_ref[...]
    lane = lax.broadcasted_iota(jnp.int32, (tm, LANES), 1)
    own = lane < HEAD_DIM
    bias_lanes = lane < HEAD_DIM + 2 * BIAS_PIECES
    for h in range(B_HEADS):
        pair = x[:, (h // 2) * LANES:(h // 2 + 1) * LANES]
        if h % 2 == 1:
            pair = pltpu.roll(pair, HEAD_DIM, 1)
        shift = (HEAD_DIM - BIAS_STRIDE * h) % LANES
        bias = pltpu.roll(slab, shift, 1) if shift else slab
        out_ref[:, h * LANES:(h + 1) * LANES] = jnp.where(
            own, pair, jnp.where(bias_lanes, bias, 0.0)).astype(BF16)


def _kvf_kernel(x_ref, g_ref, wkv_ref, wf_ref, bf_ref, place_ref, ones_ref, k_ref, v_ref, c_ref,
                carry_ref, *, tm):
    @pl.when(pl.program_id(1) == 0)
    def _():
        carry_ref[...] = jnp.zeros_like(carry_ref)

    xn = _rms(x_ref[...], g_ref[...]).astype(BF16)
    kv = jnp.dot(xn, wkv_ref[...], preferred_element_type=F32)
    vt = kv[:, B_WIDTH:].T
    pad_rows = lax.broadcasted_iota(jnp.int32, (VT_ROWS - HEAD_DIM, tm), 0)
    ones_pad = jnp.where(pad_rows == 0, 1.0, 0.0)
    for h in range(B_HEADS):
        v_ref[h * VT_ROWS:(h + 1) * VT_ROWS, :] = jnp.concatenate(
            [vt[h * HEAD_DIM:(h + 1) * HEAD_DIM], ones_pad], axis=0).astype(BF16)
    z = jnp.dot(xn, wf_ref[...], preferred_element_type=F32) + bf_ref[...]
    log_f = -(jnp.maximum(-z, 0.0) + jnp.log1p(jnp.exp(-jnp.abs(z))))
    rows = lax.broadcasted_iota(jnp.int32, (tm, LANES), 0)
    acc = log_f
    shift = 1
    while shift < tm:
        acc = acc + jnp.where(rows >= shift, pltpu.roll(acc, shift, 0), 0.0)
        shift *= 2
    acc = acc + carry_ref[...]
    c_ref[...] = acc
    carry_ref[...] = acc[tm - 1:tm, :]
    _store_augmented(kv[:, :B_WIDTH], acc, place_ref, ones_ref, k_ref)


def _kvf_proj(h3, g, wkv, wf, bf, place_k, ones_k):
    batch, seq, _ = h3.shape
    tm = ROW_TILE
    row = lambda width: pl.BlockSpec((None, tm, width), lambda b, t: (b, t, 0))
    return pl.pallas_call(
        functools.partial(_kvf_kernel, tm=tm),
        grid=(batch, seq // tm),
        in_specs=[row(D_MODEL),
                  _resident((1, D_MODEL), lambda b, t: (0, 0)),
                  _resident((D_MODEL, 2 * B_WIDTH), lambda b, t: (0, 0)),
                  _resident((D_MODEL, LANES), lambda b, t: (0, 0)),
                  _resident((1, LANES), lambda b, t: (0, 0)),
                  _resident((BIAS_PIECES * LANES, LANES), lambda b, t: (0, 0)),
                  _resident((1, LANES), lambda b, t: (0, 0))],
        out_specs=[row(AUG_WIDTH),
                   pl.BlockSpec((None, None, B_HEADS * VT_ROWS, tm), lambda b, t: (b, t, 0, 0)),
                   row(LANES)],
        out_shape=[jax.ShapeDtypeStruct((batch, seq, AUG_WIDTH), BF16),
                   jax.ShapeDtypeStruct((batch, seq // tm, B_HEADS * VT_ROWS, tm), BF16),
                   jax.ShapeDtypeStruct((batch, seq, LANES), F32)],
        scratch_shapes=[pltpu.VMEM((1, LANES), F32)],
        compiler_params=_params(2),
        name="kvf_proj",
    )(h3, g, wkv, wf, bf, place_k, ones_k)


def _qproj_kernel(x_ref, g_ref, w_ref, c_ref, place_ref, ones_ref, o_ref):
    xn = _rms(x_ref[...], g_ref[...]).astype(BF16)
    q = jnp.dot(xn, w_ref[...], preferred_element_type=F32) * (SCALE * LOG2E)
    _store_augmented(q, c_ref[...], place_ref, ones_ref, o_ref)


def _q_proj(h2, g, w, c2, place_q, ones_q):
    m = h2.shape[0]
    tm = ROW_TILE
    return pl.pallas_call(
        _qproj_kernel,
        grid=(m // tm,),
        in_specs=[pl.BlockSpec((tm, D_MODEL), lambda i: (i, 0)),
                  _resident((1, D_MODEL), lambda i: (0, 0)),
                  _resident((D_MODEL, B_WIDTH), lambda i: (0, 0)),
                  pl.BlockSpec((tm, LANES), lambda i: (i, 0)),
                  _resident((BIAS_PIECES * LANES, LANES), lambda i: (0, 0)),
                  _resident((1, LANES), lambda i: (0, 0))],
        out_specs=pl.BlockSpec((tm, AUG_WIDTH), lambda i: (i, 0)),
        out_shape=jax.ShapeDtypeStruct((m, AUG_WIDTH), BF16),
        compiler_params=_params(1),
        name="q_proj",
    )(h2, g, w, c2, place_q, ones_q)


def _fox_kernel(q_ref, k_ref, vt_ref, o_ref, *, tq, nh):
    visible = (lax.broadcasted_iota(jnp.int32, (tq, tq), 0)
               <= lax.broadcasted_iota(jnp.int32, (tq, tq), 1))

    def scores_of(qi, j):
        return [lax.dot_general(k_ref[j * tq:(j + 1) * tq, hh * LANES:(hh + 1) * LANES],
                                q_ref[qi * tq:(qi + 1) * tq, hh * LANES:(hh + 1) * LANES],
                                NT_DIMS, preferred_element_type=F32) for hh in range(nh)]

    def update(j, scores, state, diagonal):
        stats = []
        for hh in range(nh):
            m, _ = state[hh]
            s = scores[hh]
            if diagonal:
                s = jnp.where(visible, s, NEG)
            m_new = jnp.maximum(m, jnp.max(s, axis=0, keepdims=True))
            stats.append((m_new, jnp.exp2(m - m_new), jnp.exp2(s - m_new).astype(BF16)))
        out = []
        for hh in range(nh):
            m_new, alpha, p = stats[hh]
            vt = vt_ref[j, hh * VT_ROWS:(hh + 1) * VT_ROWS, :]
            out.append((m_new, alpha * state[hh][1] + jnp.dot(vt, p, preferred_element_type=F32)))
        return tuple(out)

    pairs = [(qi, j) for qi in range(q_ref.shape[0] // tq) for j in range(qi + 1)]
    scores = scores_of(*pairs[0])
    state = None
    for n, (qi, j) in enumerate(pairs):
        nxt = scores_of(*pairs[n + 1]) if n + 1 < len(pairs) else None
        if j == 0:
            state = tuple((jnp.full((1, tq), NEG, F32), jnp.zeros((VT_ROWS, tq), F32)) for _ in range(nh))
        state = update(j, scores, state, j == qi)
        if j == qi:
            out_t = jnp.concatenate([acc[:HEAD_DIM] / acc[HEAD_DIM:HEAD_DIM + 1] for _, acc in state], axis=0)
            o_ref[qi * tq:(qi + 1) * tq, :] = out_t.T.astype(BF16)
        scores = nxt


def _fox_attention(q_aug, k_aug, v_t):
    batch, nt, _, tq = v_t.shape
    assert tq == FOX_TILE
    seq = nt * tq
    nh = FOX_HEADS
    q3 = q_aug.reshape(batch, seq, AUG_WIDTH)
    return pl.pallas_call(
        functools.partial(_fox_kernel, tq=tq, nh=nh),
        grid=(batch, B_HEADS // nh),
        in_specs=[pl.BlockSpec((None, seq, nh * LANES), lambda b, hp: (b, 0, hp)),
                  pl.BlockSpec((None, seq, nh * LANES), lambda b, hp: (b, 0, hp)),
                  pl.BlockSpec((None, nt, nh * VT_ROWS, tq), lambda b, hp: (b, 0, hp, 0))],
        out_specs=pl.BlockSpec((None, seq, nh * HEAD_DIM), lambda b, hp: (b, 0, hp)),
        out_shape=jax.ShapeDtypeStruct((batch, seq, B_WIDTH), BF16),
        compiler_params=_params(2),
        name="fox_attn",
    )(q3, k_aug, v_t)


def _bias_placement():
    assert 2 * BIAS_PIECES <= BIAS_STRIDE
    place_q = np.zeros((BIAS_PIECES * LANES, LANES), np.float32)
    place_k = np.zeros((BIAS_PIECES * LANES, LANES), np.float32)
    ones_q = np.zeros((1, LANES), np.float32)
    ones_k = np.zeros((1, LANES), np.float32)
    for h in range(B_HEADS):
        base = h * BIAS_STRIDE
        for piece in range(BIAS_PIECES):
            place_q[piece * LANES + h, base + piece] = 1.0
            ones_k[0, base + piece] = 1.0
            place_k[piece * LANES + h, base + BIAS_PIECES + piece] = -1.0
            ones_q[0, base + BIAS_PIECES + piece] = 1.0
    return (jnp.asarray(place_q, BF16), jnp.asarray(ones_q), jnp.asarray(place_k, BF16), jnp.asarray(ones_k))


def _rope_lane_tables(seq):
    pos = jnp.arange(seq, dtype=F32)
    inv = ROPE_THETA ** (-jnp.arange(0, ROPE_DIM, 2, dtype=F32) / ROPE_DIM)
    ang = pos[:, None] * inv[None, :]
    cos, sin = jnp.cos(ang), jnp.sin(ang)
    ones = jnp.ones((seq, HEAD_DIM - ROPE_DIM), F32)
    zeros = jnp.zeros((seq, HEAD_DIM - ROPE_DIM), F32)
    zh = jnp.zeros((seq, ROPE_HALF), F32)
    cos_h = jnp.concatenate([cos, cos, ones], axis=1)
    sa_h = jnp.concatenate([zh, sin, zeros], axis=1)
    sb_h = jnp.concatenate([-sin, zh, zeros], axis=1)
    rep = LANES // HEAD_DIM
    return tuple(jnp.tile(t, (1, rep)) for t in (cos_h, sa_h, sb_h))


def _group_major(w_qkv):
    d = w_qkv.shape[0]
    w = w_qkv.reshape(d, 3, len(A_GROUPS), GROUP_WIDTH).transpose(0, 2, 1, 3)
    return w.reshape(d, 3 * A_WIDTH)


def kernel(x, norm_gains, w_qkv_a, w_o_a, w_q_b, w_o_b, kv_norm, w_kvf, b_f, w_up, conv_w, conv_b, w_down):
    batch, seq, _ = x.shape
    m = batch * seq
    cos_t, sa_t, sb_t = _rope_lane_tables(seq)
    place_q, ones_q, place_k, ones_k = _bias_placement()
    gains = norm_gains.reshape(DEPTH * 4, 1, D_MODEL)
    w_qkv_a, w_o_a, w_q_b, w_o_b, w_up, w_down = (
        t.astype(BF16) for t in (w_qkv_a, w_o_a, w_q_b, w_o_b, w_up, w_down))
    w_kv = w_kvf[:, :2 * B_WIDTH].astype(BF16)
    w_f = jnp.pad(w_kvf[:, 2 * B_WIDTH:], ((0, 0), (0, LANES - B_HEADS))).astype(BF16)
    b_f = jnp.pad(b_f, (0, LANES - B_HEADS)).reshape(1, LANES)
    conv_b = conv_b.reshape(DEPTH, 1, 2 * D_FF)

    h = x
    k_aug = v_sh = c_sh = None
    for l in range(DEPTH):
        g = lambda i: gains[4 * l + i]
        if l < N_A:
            qkv = _qkv_proj(h, g(0), _group_major(w_qkv_a[l]), cos_t, sa_t, sb_t)
            os_, ls_ = zip(*[_dilated_attention(qkv[gi], gi, window, r)
                             for gi, (window, r) in enumerate(A_GROUPS)])
            mix_inputs, w_o = [*os_, *ls_], w_o_a[l]
        else:
            if l == N_A:
                k_aug, v_sh, c_sh = _kvf_proj(h, kv_norm.reshape(1, D_MODEL), w_kv, w_f, b_f,
                                              place_k, ones_k)
            j = l - N_A
            q_aug = _q_proj(h.reshape(m, D_MODEL), g(0), w_q_b[j], c_sh.reshape(m, LANES),
                            place_q, ones_q)
            mix_inputs, w_o = [_fox_attention(q_aug, k_aug, v_sh)], w_o_b[j]
        h = _layer_tail(mix_inputs, w_o, h, l, gains, w_up, conv_w, conv_b, w_down)
    return h
```

```python
import functools

import numpy as np

import jax
import jax.numpy as jnp
from jax import lax
from jax.experimental import pallas as pl
from jax.experimental.pallas import tpu as pltpu

D_MODEL = 1024
DEPTH = 4
HEAD_DIM = 64
A_GROUPS = ((128, 1), (512, 4), (2048, 16))
A_HEADS_PER_GROUP = 4
A_HEADS = A_HEADS_PER_GROUP * len(A_GROUPS)
A_WIDTH = A_HEADS * HEAD_DIM
GROUP_WIDTH = A_HEADS_PER_GROUP * HEAD_DIM
B_HEADS = D_MODEL // HEAD_DIM
B_WIDTH = B_HEADS * HEAD_DIM
N_A = DEPTH // 2
D_FF = 2816
CONV_W = 3
ROPE_DIM = HEAD_DIM // 4
ROPE_HALF = ROPE_DIM // 2
ROPE_THETA = 500000.0
BLK = 128
EPS = 1e-6
NEG = -1e30
SCALE = HEAD_DIM ** -0.5
LOG2E = float(np.log2(np.e))

LANES = 128
SUBLANES = 8
MXU_DIM = 256
VMEM_LIMIT = 56 * 1024 * 1024

ROW_TILE = 512
FFN_CHUNK = MXU_DIM
TAIL_SUBTILES = 2
FFN_GROUPS = (4, 4, 3)
FOX_TILE = 512
VT_ROWS = HEAD_DIM + 16
FOX_HEADS = 4
AUG_WIDTH = B_HEADS * LANES
BIAS_PIECES = 3
BIAS_STRIDE = LANES // B_HEADS
DILATED_PLAN = ((1, 3), (4, 4), (16, 4))

F32 = jnp.float32
BF16 = jnp.bfloat16
NT_DIMS = (((1,), (1,)), ((), ()))


def _params(n_axes):
    return pltpu.CompilerParams(dimension_semantics=("arbitrary",) * n_axes,
                                vmem_limit_bytes=VMEM_LIMIT)


def _resident(shape, index_map):
    return pl.BlockSpec(shape, index_map, pipeline_mode=pl.Buffered(1))


def _rms(x, g):
    return x * lax.rsqrt(jnp.mean(x * x, axis=-1, keepdims=True) + EPS) * g


def _qkv_kernel(x_ref, g_ref, w_ref, cos_ref, sa_ref, sb_ref, o0_ref, o1_ref, o2_ref, ybuf_ref, *, tm):
    xn = _rms(x_ref[...], g_ref[...]).astype(BF16)
    y = jnp.dot(xn, w_ref[...], preferred_element_type=F32)
    cos, sa, sb = cos_ref[...], sa_ref[...], sb_ref[...]
    slabs_per_group = 3 * GROUP_WIDTH // LANES
    slabs_per_kind = GROUP_WIDTH // LANES
    for c in range(3 * A_WIDTH // LANES):
        grp, kind = c // slabs_per_group, (c % slabs_per_group) // slabs_per_kind
        t = y[:, c * LANES:(c + 1) * LANES]
        if kind < 2:
            t = t * cos + pltpu.roll(t, ROPE_HALF, 1) * sa + pltpu.roll(t, LANES - ROPE_HALF, 1) * sb
        if kind == 0:
            t = t * SCALE
        if grp == 0:
            o0_ref[0, :, c * LANES:(c + 1) * LANES] = t.astype(BF16)
        else:
            ybuf_ref[c - slabs_per_group] = t
    for grp, o_ref in ((1, o1_ref), (2, o2_ref)):
        r = A_GROUPS[grp][1]
        for j in range(r):
            for c in range(slabs_per_group):
                o_ref[j, :, c * LANES:(c + 1) * LANES] = ybuf_ref[
                    (grp - 1) * slabs_per_group + c, pl.ds(j, tm // r, stride=r), :].astype(BF16)


def _qkv_proj(h3, g, w, cos_t, sa_t, sb_t):
    batch, seq, _ = h3.shape
    tm = ROW_TILE
    n = 3 * A_WIDTH
    gw = 3 * GROUP_WIDTH
    tab = pl.BlockSpec((tm, LANES), lambda b, t: (t, 0))
    (_, r1), (_, r2) = A_GROUPS[1], A_GROUPS[2]
    return pl.pallas_call(
        functools.partial(_qkv_kernel, tm=tm),
        grid=(batch, seq // tm),
        in_specs=[pl.BlockSpec((None, tm, D_MODEL), lambda b, t: (b, t, 0)),
                  _resident((1, D_MODEL), lambda b, t: (0, 0)),
                  _resident((D_MODEL, n), lambda b, t: (0, 0)),
                  tab, tab, tab],
        out_specs=[pl.BlockSpec((None, 1, tm, gw), lambda b, t: (b, 0, t, 0)),
                   pl.BlockSpec((None, r1, tm // r1, gw), lambda b, t: (b, 0, t, 0)),
                   pl.BlockSpec((None, r2, tm // r2, gw), lambda b, t: (b, 0, t, 0))],
        out_shape=[jax.ShapeDtypeStruct((batch, 1, seq, gw), BF16),
                   jax.ShapeDtypeStruct((batch, r1, seq // r1, gw), BF16),
                   jax.ShapeDtypeStruct((batch, r2, seq // r2, gw), BF16)],
        scratch_shapes=[pltpu.VMEM((2 * gw // LANES, tm, LANES), F32)],
        compiler_params=_params(2),
        name="qkv_proj",
    )(h3, g, w, cos_t, sa_t, sb_t)


def _dilated_kernel(q_ref, k_ref, v_ref, o_ref, lse_ref, *, n_res, nb, n_back, unroll):
    nh = A_HEADS_PER_GROUP
    lane = lax.broadcasted_iota(jnp.int32, (BLK, GROUP_WIDTH), 1)
    head_masks = [(lane >= h * HEAD_DIM) & (lane < (h + 1) * HEAD_DIM) for h in range(nh)]

    def band(width):
        qi = lax.broadcasted_iota(jnp.int32, (nh * BLK, width), 0) & (BLK - 1)
        kj = lax.broadcasted_iota(jnp.int32, (nh * BLK, width), 1)
        rel = qi + (width - BLK) - kj
        return (rel >= 0) & (rel <= n_back)

    def attend(items, width):
        mask = band(width)
        scores = []
        for res, q0, k0 in items:
            q = q_ref[res, pl.ds(q0, BLK), :]
            qs = jnp.concatenate([jnp.where(hm, q, jnp.zeros_like(q)) for hm in head_masks], axis=0)
            scores.append(lax.dot_general(qs, k_ref[res, pl.ds(k0, width), :], NT_DIMS,
                                          preferred_element_type=F32))
        probs = []
        for s in scores:
            s = jnp.where(mask, s, NEG)
            m = jnp.max(s, axis=-1, keepdims=True)
            p = jnp.exp(s - m)
            l = jnp.sum(p, axis=-1, keepdims=True)
            probs.append(((p / l).astype(BF16), m + jnp.log(l)))
        for (res, q0, k0), (pn, lse) in zip(items, probs):
            pv = jnp.dot(pn, v_ref[res, pl.ds(k0, width), :], preferred_element_type=F32)
            o = pv[0:BLK]
            ls = jnp.broadcast_to(lse[0:BLK], (BLK, GROUP_WIDTH))
            for h in range(1, nh):
                o = jnp.where(head_masks[h], pv[h * BLK:(h + 1) * BLK], o)
                ls = jnp.where(head_masks[h], lse[h * BLK:(h + 1) * BLK], ls)
            o_ref[res, pl.ds(q0, BLK), :] = o.astype(BF16)
            lse_ref[res, pl.ds(q0, BLK), :] = ls

    firsts = [(res, 0, 0) for res in range(n_res)]
    for i in range(0, n_res, unroll):
        attend(firsts[i:i + unroll], BLK)
    rest = [(res, i * BLK, (i - 1) * BLK) for i in range(1, nb) for res in range(n_res)]
    for i in range(0, len(rest), unroll):
        attend(rest[i:i + unroll], 2 * BLK)


def _dilated_attention(qkv_g, g, window, r):
    batch, _, ln, _ = qkv_g.shape
    n_res, unroll = DILATED_PLAN[g]
    nb = ln // BLK
    assert ln % BLK == 0 and r % n_res == 0

    def in_spec(which):
        return pl.BlockSpec((None, n_res, ln, GROUP_WIDTH), lambda b, j: (b, j, 0, which))

    out_spec = pl.BlockSpec((None, n_res, ln, GROUP_WIDTH), lambda b, j: (b, j, 0, 0))
    return pl.pallas_call(
        functools.partial(_dilated_kernel, n_res=n_res, nb=nb, n_back=window // r, unroll=unroll),
        grid=(batch, r // n_res),
        in_specs=[in_spec(0), in_spec(1), in_spec(2)],
        out_specs=[out_spec, out_spec],
        out_shape=[jax.ShapeDtypeStruct((batch, r, ln, GROUP_WIDTH), BF16),
                   jax.ShapeDtypeStruct((batch, r, ln, GROUP_WIDTH), F32)],
        compiler_params=_params(2),
        name=f"dilated_attn_g{g}",
    )(qkv_g, qkv_g, qkv_g)


def _mix_dilated(o0_ref, o1_ref, o2_ref, l0_ref, l1_ref, l2_ref, w_ref, obuf_ref, lbuf_ref, *, tm, sub):
    slabs = GROUP_WIDTH // LANES
    for grp, (o_ref, l_ref) in enumerate(((o0_ref, l0_ref), (o1_ref, l1_ref), (o2_ref, l2_ref))):
        r = A_GROUPS[grp][1]
        src = slice(sub * tm // r, (sub + 1) * tm // r)
        for j in range(r):
            rows = pl.ds(j, tm // r, stride=r) if r > 1 else slice(None)
            for c in range(slabs):
                cols = slice(c * LANES, (c + 1) * LANES)
                obuf_ref[grp * slabs + c, rows, :] = o_ref[j, src, cols].astype(F32)
                lbuf_ref[grp * slabs + c, rows, :] = l_ref[j, src, cols]

    def group(buf_ref, grp):
        return jnp.concatenate([buf_ref[grp * slabs + c] for c in range(slabs)], axis=1)

    l0, l1, l2 = (group(lbuf_ref, grp) for grp in range(3))
    mx = jnp.maximum(jnp.maximum(l0, l1), l2)
    e0, e1, e2 = jnp.exp(l0 - mx), jnp.exp(l1 - mx), jnp.exp(l2 - mx)
    den = e0 + e1 + e2
    mixed = jnp.concatenate([(group(obuf_ref, grp) * (e / den)).astype(BF16)
                             for grp, e in enumerate((e0, e1, e2))], axis=1)
    return jnp.dot(mixed, w_ref[...], preferred_element_type=F32)


def _to_strided_rows(x, buf_ref):
    tm, d = x.shape
    n = tm // SUBLANES
    pitch = buf_ref.shape[1] // SUBLANES
    slabs = d // LANES
    for c in range(slabs):
        for s in range(SUBLANES):
            buf_ref[c, s * pitch:s * pitch + n, :] = x[s * n:(s + 1) * n, c * LANES:(c + 1) * LANES]
    return jnp.concatenate(
        [jnp.concatenate([buf_ref[c, pl.ds(i, SUBLANES, stride=pitch), :] for c in range(slabs)], axis=1)
         for i in range(n)], axis=0)


def _store_token_rows(y, buf_ref, out_ref, row0):
    tm, d = y.shape
    n = tm // SUBLANES
    per = n // SUBLANES
    slabs = d // LANES
    for c in range(slabs):
        buf_ref[c] = y[:, c * LANES:(c + 1) * LANES]
    for j in range(n):
        start = SUBLANES * SUBLANES * (j % per) + j // per
        out_ref[row0 + j * SUBLANES:row0 + (j + 1) * SUBLANES, :] = jnp.concatenate(
            [buf_ref[c, pl.ds(start, SUBLANES, stride=SUBLANES), :] for c in range(slabs)], axis=1)


def _tail_kernel(*refs, tm, n_mix, n_sub):
    mix_refs, refs = refs[:n_mix + 1], refs[n_mix + 1:]
    h_ref, g1_ref, g2_ref, wup_ref, cw_ref, cb_ref, wdn_ref, g3_ref, out_ref, carry_ref = refs[:10]
    scratch = refs[10:]
    per_sub = len(FFN_GROUPS) + 2
    extra = scratch[n_sub * per_sub:]

    @pl.when(pl.program_id(1) == 0)
    def _():
        carry_ref[...] = jnp.zeros_like(carry_ref)

    ends = {}
    subs = [_tail_sub(mix_refs, extra, h_ref, g1_ref, g2_ref, wup_ref, cw_ref, cb_ref, wdn_ref, g3_ref,
                      out_ref, carry_ref, scratch[i * per_sub:(i + 1) * per_sub], ends,
                      tm=tm, sub=i, n_sub=n_sub) for i in range(n_sub)]
    live = []
    while subs or live:
        if subs:
            live.append(subs.pop(0))
        for gen in list(live):
            if next(gen, "done") == "done":
                live.remove(gen)


def _tail_sub(mix_refs, extra, h_ref, g1_ref, g2_ref, wup_ref, cw_ref, cb_ref, wdn_ref, g3_ref, out_ref,
              carry_ref, own, ends, *, tm, sub, n_sub):
    hmids, pin_ref, pout_ref = own[:len(FFN_GROUPS)], own[-2], own[-1]
    rows = slice(sub * tm, (sub + 1) * tm)
    if len(mix_refs) == 2:
        o_ref, w_ref = mix_refs
        mix = jnp.dot(o_ref[rows, :], w_ref[...], preferred_element_type=F32)
    else:
        mix = _mix_dilated(*mix_refs, *extra, tm=tm, sub=sub)
    h = _to_strided_rows(h_ref[rows, :] + _rms(mix, g1_ref[...]), pin_ref)
    xn = _rms(h, g2_ref[...]).astype(BF16)
    last_sublane = lax.broadcasted_iota(jnp.int32, (SUBLANES, FFN_CHUNK), 0) == SUBLANES - 1
    yield

    def conv(col, k):
        cs = slice(col, col + FFN_CHUNK)
        a = jnp.dot(xn, wup_ref[:, cs], preferred_element_type=F32)
        prev = carry_ref[k] if sub == 0 else ends.pop((sub - 1, k))
        if sub == n_sub - 1:
            carry_ref[k] = a[tm - 2 * SUBLANES:tm]
        else:
            ends[(sub, k)] = a[tm - 2 * SUBLANES:tm]
        wrapped = [pltpu.roll(jnp.where(last_sublane, prev[g * SUBLANES:(g + 1) * SUBLANES],
                                        a[tm - (2 - g) * SUBLANES:tm - (1 - g) * SUBLANES]), 1, 0)
                   for g in range(2)]
        a1 = jnp.concatenate([wrapped[1], a[:tm - SUBLANES]], axis=0)
        a2 = jnp.concatenate([wrapped[0], wrapped[1], a[:tm - 2 * SUBLANES]], axis=0)
        return (a2 * cw_ref[0:1, cs] + a1 * cw_ref[1:2, cs] + a * cw_ref[2:3, cs]) + cb_ref[:, cs]

    f = None
    first = 0
    for group, hmid_ref in zip(FFN_GROUPS, hmids):
        for i in range(group):
            c = first + i
            gate = conv(c * FFN_CHUNK, 2 * c)
            val = conv(D_FF + c * FFN_CHUNK, 2 * c + 1)
            hmid_ref[:, i * FFN_CHUNK:(i + 1) * FFN_CHUNK] = (
                jax.nn.gelu(gate, approximate=True) * val).astype(BF16)
            if i == group - 1:
                ks = slice(first * FFN_CHUNK, (first + group) * FFN_CHUNK)
                part = jnp.dot(hmid_ref[...], wdn_ref[ks, :], preferred_element_type=F32)
                f = part if f is None else f + part
            yield
        first += group
    _store_token_rows(h + _rms(f, g3_ref[...]), pout_ref, out_ref, sub * tm)


def _layer_tail(mix_inputs, w_o, h3, layer, g_all, w_up, cw, cb, w_down):
    batch, seq, _ = h3.shape
    n_sub = TAIL_SUBTILES
    tm = ROW_TILE // n_sub
    blk = ROW_TILE
    row = pl.BlockSpec((None, blk, D_MODEL), lambda b, t: (b, t, 0))
    gain = lambda i: _resident((None, 1, D_MODEL), lambda b, t: (4 * layer + i, 0, 0))
    if len(mix_inputs) == 1:
        mix_specs = [pl.BlockSpec((None, blk, B_WIDTH), lambda b, t: (b, t, 0))]
        extra_scratch = []
    else:
        mix_specs = 2 * [pl.BlockSpec((None, r, blk // r, GROUP_WIDTH), lambda b, t: (b, 0, t, 0))
                         for _, r in A_GROUPS]
        extra_scratch = 2 * [pltpu.VMEM((A_WIDTH // LANES, tm, LANES), F32)]
    return pl.pallas_call(
        functools.partial(_tail_kernel, tm=tm, n_mix=len(mix_inputs), n_sub=n_sub),
        grid=(batch, seq // blk),
        in_specs=mix_specs + [_resident(w_o.shape, lambda b, t: (0, 0)),
                              row, gain(1), gain(2),
                              _resident((None, D_MODEL, 2 * D_FF), lambda b, t: (layer, 0, 0)),
                              _resident((None, CONV_W, 2 * D_FF), lambda b, t: (layer, 0, 0)),
                              _resident((None, 1, 2 * D_FF), lambda b, t: (layer, 0, 0)),
                              _resident((None, D_FF, D_MODEL), lambda b, t: (layer, 0, 0)),
                              gain(3)],
        out_specs=row,
        out_shape=jax.ShapeDtypeStruct(h3.shape, F32),
        scratch_shapes=([pltpu.VMEM((2 * D_FF // FFN_CHUNK, 2 * SUBLANES, FFN_CHUNK), F32)]
                        + n_sub * ([pltpu.VMEM((tm, group * FFN_CHUNK), BF16) for group in FFN_GROUPS]
                                   + [pltpu.VMEM((D_MODEL // LANES, tm + SUBLANES * SUBLANES, LANES), F32),
                                      pltpu.VMEM((D_MODEL // LANES, tm, LANES), F32)])
                        + extra_scratch),
        compiler_params=_params(2),
        name="layer_tail",
    )(*mix_inputs, w_o, h3, g_all, g_all, w_up, cw, cb, w_down, g_all)


def _split_bf16(c):
    pieces, rest = [], c
    for _ in range(BIAS_PIECES):
        piece = rest.astype(BF16)
        pieces.append(piece)
        rest = rest - piece.astype(F32)
    return jnp.concatenate(pieces, axis=1)


def _store_augmented(x, c_tile, place_ref, ones_ref, out_ref):
    tm = x.shape[0]
    slab = jnp.dot(_split_bf16(c_tile * LOG2E), place_ref[...], preferred_element_type=F32) + ones_ref[...]
    lane = lax.broadcasted_iota(jnp.int32, (tm, LANES), 1)
    own = lane < HEAD_DIM
    bias_lanes = lane < HEAD_DIM + 2 * BIAS_PIECES
    for h in range(B_HEADS):
        pair = x[:, (h // 2) * LANES:(h // 2 + 1) * LANES]
        if h % 2 == 1:
            pair = pltpu.roll(pair, HEAD_DIM, 1)
        shift = (HEAD_DIM - BIAS_STRIDE * h) % LANES
        bias = pltpu.roll(slab, shift, 1) if shift else slab
        out_ref[:, h * LANES:(h + 1) * LANES] = jnp.where(
            own, pair, jnp.where(bias_lanes, bias, 0.0)).astype(BF16)


def _kvf_kernel(x_ref, g_ref, wkv_ref, wf_ref, bf_ref, place_ref, ones_ref, k_ref, v_ref, c_ref,
                carry_ref, *, tm):
    @pl.when(pl.program_id(1) == 0)
    def _():
        carry_ref[...] = jnp.zeros_like(carry_ref)

    xn = _rms(x_ref[...], g_ref[...]).astype(BF16)
    kv = jnp.dot(xn, wkv_ref[...], preferred_element_type=F32)
    vt = kv[:, B_WIDTH:].T
    pad_rows = lax.broadcasted_iota(jnp.int32, (VT_ROWS - HEAD_DIM, tm), 0)
    ones_pad = jnp.where(pad_rows == 0, 1.0, 0.0)
    for h in range(B_HEADS):
        v_ref[h * VT_ROWS:(h + 1) * VT_ROWS, :] = jnp.concatenate(
            [vt[h * HEAD_DIM:(h + 1) * HEAD_DIM], ones_pad], axis=0).astype(BF16)
    z = jnp.dot(xn, wf_ref[...], preferred_element_type=F32) + bf_ref[...]
    log_f = -(jnp.maximum(-z, 0.0) + jnp.log1p(jnp.exp(-jnp.abs(z))))
    rows = lax.broadcasted_iota(jnp.int32, (tm, LANES), 0)
    acc = log_f
    shift = 1
    while shift < tm:
        acc = acc + jnp.where(rows >= shift, pltpu.roll(acc, shift, 0), 0.0)
        shift *= 2
    acc = acc + carry_ref[...]
    c_ref[...] = acc
    carry_ref[...] = acc[tm - 1:tm, :]
    _store_augmented(kv[:, :B_WIDTH], acc, place_ref, ones_ref, k_ref)


def _kvf_proj(h3, g, wkv, wf, bf, place_k, ones_k):
    batch, seq, _ = h3.shape
    tm = ROW_TILE
    row = lambda width: pl.BlockSpec((None, tm, width), lambda b, t: (b, t, 0))
    return pl.pallas_call(
        functools.partial(_kvf_kernel, tm=tm),
        grid=(batch, seq // tm),
        in_specs=[row(D_MODEL),
                  _resident((1, D_MODEL), lambda b, t: (0, 0)),
                  _resident((D_MODEL, 2 * B_WIDTH), lambda b, t: (0, 0)),
                  _resident((D_MODEL, LANES), lambda b, t: (0, 0)),
                  _resident((1, LANES), lambda b, t: (0, 0)),
                  _resident((BIAS_PIECES * LANES, LANES), lambda b, t: (0, 0)),
                  _resident((1, LANES), lambda b, t: (0, 0))],
        out_specs=[row(AUG_WIDTH),
                   pl.BlockSpec((None, None, B_HEADS * VT_ROWS, tm), lambda b, t: (b, t, 0, 0)),
                   row(LANES)],
        out_shape=[jax.ShapeDtypeStruct((batch, seq, AUG_WIDTH), BF16),
                   jax.ShapeDtypeStruct((batch, seq // tm, B_HEADS * VT_ROWS, tm), BF16),
                   jax.ShapeDtypeStruct((batch, seq, LANES), F32)],
        scratch_shapes=[pltpu.VMEM((1, LANES), F32)],
        compiler_params=_params(2),
        name="kvf_proj",
    )(h3, g, wkv, wf, bf, place_k, ones_k)


def _qproj_kernel(x_ref, g_ref, w_ref, c_ref, place_ref, ones_ref, o_ref):
    xn = _rms(x_ref[...], g_ref[...]).astype(BF16)
    q = jnp.dot(xn, w_ref[...], preferred_element_type=F32) * (SCALE * LOG2E)
    _store_augmented(q, c_ref[...], place_ref, ones_ref, o_ref)


def _q_proj(h2, g, w, c2, place_q, ones_q):
    m = h2.shape[0]
    tm = ROW_TILE
    return pl.pallas_call(
        _qproj_kernel,
        grid=(m // tm,),
        in_specs=[pl.BlockSpec((tm, D_MODEL), lambda i: (i, 0)),
                  _resident((1, D_MODEL), lambda i: (0, 0)),
                  _resident((D_MODEL, B_WIDTH), lambda i: (0, 0)),
                  pl.BlockSpec((tm, LANES), lambda i: (i, 0)),
                  _resident((BIAS_PIECES * LANES, LANES), lambda i: (0, 0)),
                  _resident((1, LANES), lambda i: (0, 0))],
        out_specs=pl.BlockSpec((tm, AUG_WIDTH), lambda i: (i, 0)),
        out_shape=jax.ShapeDtypeStruct((m, AUG_WIDTH), BF16),
        compiler_params=_params(1),
        name="q_proj",
    )(h2, g, w, c2, place_q, ones_q)


def _fox_kernel(q_ref, k_ref, vt_ref, o_ref, *, tq, nh):
    half = tq // 2

    def scores_of(qi, j, key0, nkeys, query0, diagonal):
        return [lax.dot_general(k_ref[j * tq + key0:j * tq + key0 + nkeys, hh * LANES:(hh + 1) * LANES],
                                q_ref[qi * tq + query0:(qi + 1) * tq, hh * LANES:(hh + 1) * LANES],
                                NT_DIMS, preferred_element_type=F32) for hh in range(nh)]

    def update(step, scores, state):
        _, j, key0, nkeys, query0, diagonal = step
        stats = []
        for hh in range(nh):
            m = state[hh][0][:, query0:]
            s = scores[hh]
            if diagonal:
                visible = (lax.broadcasted_iota(jnp.int32, s.shape, 0)
                           <= lax.broadcasted_iota(jnp.int32, s.shape, 1))
                s = jnp.where(visible, s, NEG)
            m_new = jnp.maximum(m, jnp.max(s, axis=0, keepdims=True))
            stats.append((m_new, jnp.exp2(m - m_new), jnp.exp2(s - m_new).astype(BF16)))
        out = []
        for hh in range(nh):
            m_new, alpha, p = stats[hh]
            vt = vt_ref[j, hh * VT_ROWS:(hh + 1) * VT_ROWS, key0:key0 + nkeys]
            m_old, acc_old = state[hh]
            acc = alpha * acc_old[:, query0:] + jnp.dot(vt, p, preferred_element_type=F32)
            if query0:
                m_new = jnp.concatenate([m_old[:, :query0], m_new], axis=1)
                acc = jnp.concatenate([acc_old[:, :query0], acc], axis=1)
            out.append((m_new, acc))
        return tuple(out)

    steps = []
    for qi in range(q_ref.shape[0] // tq):
        steps += [(qi, j, 0, tq, 0, False) for j in range(qi)]
        steps += [(qi, qi, 0, half, 0, True), (qi, qi, half, half, half, True)]
    scores = scores_of(*steps[0])
    state = None
    for n, step in enumerate(steps):
        nxt = scores_of(*steps[n + 1]) if n + 1 < len(steps) else None
        qi, j, key0 = step[:3]
        if j == 0 and key0 == 0:
            state = tuple((jnp.full((1, tq), NEG, F32), jnp.zeros((VT_ROWS, tq), F32)) for _ in range(nh))
        state = update(step, scores, state)
        if j == qi and key0:
            out_t = jnp.concatenate([acc[:HEAD_DIM] / acc[HEAD_DIM:HEAD_DIM + 1] for _, acc in state], axis=0)
            o_ref[qi * tq:(qi + 1) * tq, :] = out_t.T.astype(BF16)
        scores = nxt


def _fox_attention(q_aug, k_aug, v_t):
    batch, nt, _, tq = v_t.shape
    assert tq == FOX_TILE
    seq = nt * tq
    nh = FOX_HEADS
    q3 = q_aug.reshape(batch, seq, AUG_WIDTH)
    return pl.pallas_call(
        functools.partial(_fox_kernel, tq=tq, nh=nh),
        grid=(batch, B_HEADS // nh),
        in_specs=[pl.BlockSpec((None, seq, nh * LANES), lambda b, hp: (b, 0, hp)),
                  pl.BlockSpec((None, seq, nh * LANES), lambda b, hp: (b, 0, hp)),
                  pl.BlockSpec((None, nt, nh * VT_ROWS, tq), lambda b, hp: (b, 0, hp, 0))],
        out_specs=pl.BlockSpec((None, seq, nh * HEAD_DIM), lambda b, hp: (b, 0, hp)),
        out_shape=jax.ShapeDtypeStruct((batch, seq, B_WIDTH), BF16),
        compiler_params=_params(2),
        name="fox_attn",
    )(q3, k_aug, v_t)


def _bias_placement():
    assert 2 * BIAS_PIECES <= BIAS_STRIDE
    place_q = np.zeros((BIAS_PIECES * LANES, LANES), np.float32)
    place_k = np.zeros((BIAS_PIECES * LANES, LANES), np.float32)
    ones_q = np.zeros((1, LANES), np.float32)
    ones_k = np.zeros((1, LANES), np.float32)
    for h in range(B_HEADS):
        base = h * BIAS_STRIDE
        for piece in range(BIAS_PIECES):
            place_q[piece * LANES + h, base + piece] = 1.0
            ones_k[0, base + piece] = 1.0
            place_k[piece * LANES + h, base + BIAS_PIECES + piece] = -1.0
            ones_q[0, base + BIAS_PIECES + piece] = 1.0
    return (jnp.asarray(place_q, BF16), jnp.asarray(ones_q), jnp.asarray(place_k, BF16), jnp.asarray(ones_k))


def _rope_lane_tables(seq):
    pos = jnp.arange(seq, dtype=F32)
    inv = ROPE_THETA ** (-jnp.arange(0, ROPE_DIM, 2, dtype=F32) / ROPE_DIM)
    ang = pos[:, None] * inv[None, :]
    cos, sin = jnp.cos(ang), jnp.sin(ang)
    ones = jnp.ones((seq, HEAD_DIM - ROPE_DIM), F32)
    zeros = jnp.zeros((seq, HEAD_DIM - ROPE_DIM), F32)
    zh = jnp.zeros((seq, ROPE_HALF), F32)
    cos_h = jnp.concatenate([cos, cos, ones], axis=1)
    sa_h = jnp.concatenate([zh, sin, zeros], axis=1)
    sb_h = jnp.concatenate([-sin, zh, zeros], axis=1)
    rep = LANES // HEAD_DIM
    return tuple(jnp.tile(t, (1, rep)) for t in (cos_h, sa_h, sb_h))


def _group_major(w_qkv):
    d = w_qkv.shape[0]
    w = w_qkv.reshape(d, 3, len(A_GROUPS), GROUP_WIDTH).transpose(0, 2, 1, 3)
    return w.reshape(d, 3 * A_WIDTH)


def kernel(x, norm_gains, w_qkv_a, w_o_a, w_q_b, w_o_b, kv_norm, w_kvf, b_f, w_up, conv_w, conv_b, w_down):
    batch, seq, _ = x.shape
    m = batch * seq
    cos_t, sa_t, sb_t = _rope_lane_tables(seq)
    place_q, ones_q, place_k, ones_k = _bias_placement()
    gains = norm_gains.reshape(DEPTH * 4, 1, D_MODEL)
    w_qkv_a, w_o_a, w_q_b, w_o_b, w_up, w_down = (
        t.astype(BF16) for t in (w_qkv_a, w_o_a, w_q_b, w_o_b, w_up, w_down))
    w_kv = w_kvf[:, :2 * B_WIDTH].astype(BF16)
    w_f = jnp.pad(w_kvf[:, 2 * B_WIDTH:], ((0, 0), (0, LANES - B_HEADS))).astype(BF16)
    b_f = jnp.pad(b_f, (0, LANES - B_HEADS)).reshape(1, LANES)
    conv_b = conv_b.reshape(DEPTH, 1, 2 * D_FF)

    h = x
    k_aug = v_sh = c_sh = None
    for l in range(DEPTH):
        g = lambda i: gains[4 * l + i]
        if l < N_A:
            qkv = _qkv_proj(h, g(0), _group_major(w_qkv_a[l]), cos_t, sa_t, sb_t)
            os_, ls_ = zip(*[_dilated_attention(qkv[gi], gi, window, r)
                             for gi, (window, r) in enumerate(A_GROUPS)])
            mix_inputs, w_o = [*os_, *ls_], w_o_a[l]
        else:
            if l == N_A:
                k_aug, v_sh, c_sh = _kvf_proj(h, kv_norm.reshape(1, D_MODEL), w_kv, w_f, b_f,
                                              place_k, ones_k)
            j = l - N_A
            q_aug = _q_proj(h.reshape(m, D_MODEL), g(0), w_q_b[j], c_sh.reshape(m, LANES),
                            place_q, ones_q)
            mix_inputs, w_o = [_fox_attention(q_aug, k_aug, v_sh)], w_o_b[j]
        h = _layer_tail(mix_inputs, w_o, h, l, gains, w_up, conv_w, conv_b, w_down)
    return h
```

```python
import functools

import numpy as np

import jax
import jax.numpy as jnp
from jax import lax
from jax.experimental import pallas as pl
from jax.experimental.pallas import tpu as pltpu

D_MODEL = 1024
DEPTH = 4
HEAD_DIM = 64
A_GROUPS = ((128, 1), (512, 4), (2048, 16))
A_HEADS_PER_GROUP = 4
A_HEADS = A_HEADS_PER_GROUP * len(A_GROUPS)
A_WIDTH = A_HEADS * HEAD_DIM
GROUP_WIDTH = A_HEADS_PER_GROUP * HEAD_DIM
B_HEADS = D_MODEL // HEAD_DIM
B_WIDTH = B_HEADS * HEAD_DIM
N_A = DEPTH // 2
D_FF = 2816
CONV_W = 3
ROPE_DIM = HEAD_DIM // 4
ROPE_HALF = ROPE_DIM // 2
ROPE_THETA = 500000.0
BLK = 128
EPS = 1e-6
NEG = -1e30
SCALE = HEAD_DIM ** -0.5
LOG2E = float(np.log2(np.e))

LANES = 128
SUBLANES = 8
MXU_DIM = 256
VMEM_LIMIT = 56 * 1024 * 1024

ROW_TILE = 512
FFN_CHUNK = MXU_DIM
TAIL_SUBTILES = 2
FFN_GROUPS = (4, 4, 3)
FOX_TILE = 512
VT_ROWS = HEAD_DIM + 16
FOX_HEADS = 4
AUG_WIDTH = B_HEADS * LANES
BIAS_PIECES = 3
BIAS_STRIDE = LANES // B_HEADS
DILATED_PLAN = ((1, 3), (4, 4), (16, 4))

F32 = jnp.float32
BF16 = jnp.bfloat16
NT_DIMS = (((1,), (1,)), ((), ()))


def _params(n_axes):
    return pltpu.CompilerParams(dimension_semantics=("arbitrary",) * n_axes,
                                vmem_limit_bytes=VMEM_LIMIT)


def _resident(shape, index_map):
    return pl.BlockSpec(shape, index_map, pipeline_mode=pl.Buffered(1))


def _rms(x, g):
    return x * lax.rsqrt(jnp.mean(x * x, axis=-1, keepdims=True) + EPS) * g


def _qkv_kernel(x_ref, g_ref, w_ref, cos_ref, sa_ref, sb_ref, o0_ref, o1_ref, o2_ref, ybuf_ref, *, tm):
    xn = _rms(x_ref[...], g_ref[...]).astype(BF16)
    y = jnp.dot(xn, w_ref[...], preferred_element_type=F32)
    cos, sa, sb = cos_ref[...], sa_ref[...], sb_ref[...]
    slabs_per_group = 3 * GROUP_WIDTH // LANES
    slabs_per_kind = GROUP_WIDTH // LANES
    for c in range(3 * A_WIDTH // LANES):
        grp, kind = c // slabs_per_group, (c % slabs_per_group) // slabs_per_kind
        src = kind * A_WIDTH + grp * GROUP_WIDTH + (c % slabs_per_kind) * LANES
        t = y[:, src:src + LANES]
        if kind < 2:
            t = t * cos + pltpu.roll(t, ROPE_HALF, 1) * sa + pltpu.roll(t, LANES - ROPE_HALF, 1) * sb
        if kind == 0:
            t = t * SCALE
        if grp == 0:
            o0_ref[0, :, c * LANES:(c + 1) * LANES] = t.astype(BF16)
        else:
            ybuf_ref[c - slabs_per_group] = t
    for grp, o_ref in ((1, o1_ref), (2, o2_ref)):
        r = A_GROUPS[grp][1]
        for j in range(r):
            for c in range(slabs_per_group):
                o_ref[j, :, c * LANES:(c + 1) * LANES] = ybuf_ref[
                    (grp - 1) * slabs_per_group + c, pl.ds(j, tm // r, stride=r), :].astype(BF16)


def _qkv_proj(h3, g, w_all, idx, cos_t, sa_t, sb_t):
    batch, seq, _ = h3.shape
    tm = ROW_TILE
    n = 3 * A_WIDTH
    gw = 3 * GROUP_WIDTH
    tab = pl.BlockSpec((tm, LANES), lambda b, t: (t, 0))
    (_, r1), (_, r2) = A_GROUPS[1], A_GROUPS[2]
    return pl.pallas_call(
        functools.partial(_qkv_kernel, tm=tm),
        grid=(batch, seq // tm),
        in_specs=[pl.BlockSpec((None, tm, D_MODEL), lambda b, t: (b, t, 0)),
                  _resident((1, D_MODEL), lambda b, t: (0, 0)),
                  _resident((None, D_MODEL, n), lambda b, t: (idx, 0, 0)),
                  tab, tab, tab],
        out_specs=[pl.BlockSpec((None, 1, tm, gw), lambda b, t: (b, 0, t, 0)),
                   pl.BlockSpec((None, r1, tm // r1, gw), lambda b, t: (b, 0, t, 0)),
                   pl.BlockSpec((None, r2, tm // r2, gw), lambda b, t: (b, 0, t, 0))],
        out_shape=[jax.ShapeDtypeStruct((batch, 1, seq, gw), BF16),
                   jax.ShapeDtypeStruct((batch, r1, seq // r1, gw), BF16),
                   jax.ShapeDtypeStruct((batch, r2, seq // r2, gw), BF16)],
        scratch_shapes=[pltpu.VMEM((2 * gw // LANES, tm, LANES), F32)],
        compiler_params=_params(2),
        name="qkv_proj",
    )(h3, g, w_all, cos_t, sa_t, sb_t)


def _dilated_kernel(q_ref, k_ref, v_ref, o_ref, lse_ref, *, n_res, nb, n_back, unroll):
    nh = A_HEADS_PER_GROUP
    lane = lax.broadcasted_iota(jnp.int32, (BLK, GROUP_WIDTH), 1)
    head_masks = [(lane >= h * HEAD_DIM) & (lane < (h + 1) * HEAD_DIM) for h in range(nh)]

    def band(width):
        qi = lax.broadcasted_iota(jnp.int32, (nh * BLK, width), 0) & (BLK - 1)
        kj = lax.broadcasted_iota(jnp.int32, (nh * BLK, width), 1)
        rel = qi + (width - BLK) - kj
        return (rel >= 0) & (rel <= n_back)

    def attend(items, width):
        mask = band(width)
        scores = []
        for res, q0, k0 in items:
            q = q_ref[res, pl.ds(q0, BLK), :]
            qs = jnp.concatenate([jnp.where(hm, q, jnp.zeros_like(q)) for hm in head_masks], axis=0)
            scores.append(lax.dot_general(qs, k_ref[res, pl.ds(k0, width), :], NT_DIMS,
                                          preferred_element_type=F32))
        probs = []
        for s in scores:
            s = jnp.where(mask, s, NEG)
            m = jnp.max(s, axis=-1, keepdims=True)
            p = jnp.exp(s - m)
            l = jnp.sum(p, axis=-1, keepdims=True)
            probs.append(((p / l).astype(BF16), m + jnp.log(l)))
        for (res, q0, k0), (pn, lse) in zip(items, probs):
            pv = jnp.dot(pn, v_ref[res, pl.ds(k0, width), :], preferred_element_type=F32)
            o = pv[0:BLK]
            ls = jnp.broadcast_to(lse[0:BLK], (BLK, GROUP_WIDTH))
            for h in range(1, nh):
                o = jnp.where(head_masks[h], pv[h * BLK:(h + 1) * BLK], o)
                ls = jnp.where(head_masks[h], lse[h * BLK:(h + 1) * BLK], ls)
            o_ref[res, pl.ds(q0, BLK), :] = o.astype(BF16)
            lse_ref[res, pl.ds(q0, BLK), :] = ls

    firsts = [(res, 0, 0) for res in range(n_res)]
    for i in range(0, n_res, unroll):
        attend(firsts[i:i + unroll], BLK)
    rest = [(res, i * BLK, (i - 1) * BLK) for i in range(1, nb) for res in range(n_res)]
    for i in range(0, len(rest), unroll):
        attend(rest[i:i + unroll], 2 * BLK)


def _dilated_attention(qkv_g, g, window, r):
    batch, _, ln, _ = qkv_g.shape
    n_res, unroll = DILATED_PLAN[g]
    nb = ln // BLK
    assert ln % BLK == 0 and r % n_res == 0

    def in_spec(which):
        return pl.BlockSpec((None, n_res, ln, GROUP_WIDTH), lambda b, j: (b, j, 0, which))

    out_spec = pl.BlockSpec((None, n_res, ln, GROUP_WIDTH), lambda b, j: (b, j, 0, 0))
    return pl.pallas_call(
        functools.partial(_dilated_kernel, n_res=n_res, nb=nb, n_back=window // r, unroll=unroll),
        grid=(batch, r // n_res),
        in_specs=[in_spec(0), in_spec(1), in_spec(2)],
        out_specs=[out_spec, out_spec],
        out_shape=[jax.ShapeDtypeStruct((batch, r, ln, GROUP_WIDTH), BF16),
                   jax.ShapeDtypeStruct((batch, r, ln, GROUP_WIDTH), F32)],
        compiler_params=_params(2),
        name=f"dilated_attn_g{g}",
    )(qkv_g, qkv_g, qkv_g)


def _mix_dilated(o0_ref, o1_ref, o2_ref, l0_ref, l1_ref, l2_ref, w_ref, obuf_ref, lbuf_ref, *, tm, sub):
    slabs = GROUP_WIDTH // LANES
    for grp, (o_ref, l_ref) in enumerate(((o0_ref, l0_ref), (o1_ref, l1_ref), (o2_ref, l2_ref))):
        r = A_GROUPS[grp][1]
        src = slice(sub * tm // r, (sub + 1) * tm // r)
        for j in range(r):
            rows = pl.ds(j, tm // r, stride=r) if r > 1 else slice(None)
            for c in range(slabs):
                cols = slice(c * LANES, (c + 1) * LANES)
                obuf_ref[grp * slabs + c, rows, :] = o_ref[j, src, cols].astype(F32)
                lbuf_ref[grp * slabs + c, rows, :] = l_ref[j, src, cols]

    def group(buf_ref, grp):
        return jnp.concatenate([buf_ref[grp * slabs + c] for c in range(slabs)], axis=1)

    l0, l1, l2 = (group(lbuf_ref, grp) for grp in range(3))
    mx = jnp.maximum(jnp.maximum(l0, l1), l2)
    e0, e1, e2 = jnp.exp(l0 - mx), jnp.exp(l1 - mx), jnp.exp(l2 - mx)
    den = e0 + e1 + e2
    mixed = jnp.concatenate([(group(obuf_ref, grp) * (e / den)).astype(BF16)
                             for grp, e in enumerate((e0, e1, e2))], axis=1)
    return jnp.dot(mixed, w_ref[...], preferred_element_type=F32)


def _to_strided_rows(x, buf_ref):
    tm, d = x.shape
    n = tm // SUBLANES
    pitch = buf_ref.shape[1] // SUBLANES
    slabs = d // LANES
    for c in range(slabs):
        for s in range(SUBLANES):
            buf_ref[c, s * pitch:s * pitch + n, :] = x[s * n:(s + 1) * n, c * LANES:(c + 1) * LANES]
    return jnp.concatenate(
        [jnp.concatenate([buf_ref[c, pl.ds(i, SUBLANES, stride=pitch), :] for c in range(slabs)], axis=1)
         for i in range(n)], axis=0)


def _store_token_rows(y, buf_ref, out_ref, row0):
    tm, d = y.shape
    n = tm // SUBLANES
    per = n // SUBLANES
    slabs = d // LANES
    for c in range(slabs):
        buf_ref[c] = y[:, c * LANES:(c + 1) * LANES]
    for j in range(n):
        start = SUBLANES * SUBLANES * (j % per) + j // per
        out_ref[row0 + j * SUBLANES:row0 + (j + 1) * SUBLANES, :] = jnp.concatenate(
            [buf_ref[c, pl.ds(start, SUBLANES, stride=SUBLANES), :] for c in range(slabs)], axis=1)


def _tail_kernel(*refs, tm, n_mix, n_sub):
    mix_refs, refs = refs[:n_mix + 1], refs[n_mix + 1:]
    h_ref, g1_ref, g2_ref, wup_ref, cw_ref, cb_ref, wdn_ref, g3_ref, out_ref, carry_ref = refs[:10]
    scratch = refs[10:]
    per_sub = len(FFN_GROUPS) + 2
    extra = scratch[n_sub * per_sub:]

    @pl.when(pl.program_id(1) == 0)
    def _():
        carry_ref[...] = jnp.zeros_like(carry_ref)

    ends = {}
    subs = [_tail_sub(mix_refs, extra, h_ref, g1_ref, g2_ref, wup_ref, cw_ref, cb_ref, wdn_ref, g3_ref,
                      out_ref, carry_ref, scratch[i * per_sub:(i + 1) * per_sub], ends,
                      tm=tm, sub=i, n_sub=n_sub) for i in range(n_sub)]
    live = []
    while subs or live:
        if subs:
            live.append(subs.pop(0))
        for gen in list(live):
            if next(gen, "done") == "done":
                live.remove(gen)


def _tail_sub(mix_refs, extra, h_ref, g1_ref, g2_ref, wup_ref, cw_ref, cb_ref, wdn_ref, g3_ref, out_ref,
              carry_ref, own, ends, *, tm, sub, n_sub):
    hmids, pin_ref, pout_ref = own[:len(FFN_GROUPS)], own[-2], own[-1]
    rows = slice(sub * tm, (sub + 1) * tm)
    if len(mix_refs) == 2:
        o_ref, w_ref = mix_refs
        mix = jnp.dot(o_ref[rows, :], w_ref[...], preferred_element_type=F32)
    else:
        mix = _mix_dilated(*mix_refs, *extra, tm=tm, sub=sub)
    h = _to_strided_rows(h_ref[rows, :] + _rms(mix, g1_ref[...]), pin_ref)
    xn = _rms(h, g2_ref[...]).astype(BF16)
    last_sublane = lax.broadcasted_iota(jnp.int32, (SUBLANES, FFN_CHUNK), 0) == SUBLANES - 1
    yield

    def conv(col, k):
        cs = slice(col, col + FFN_CHUNK)
        a = jnp.dot(xn, wup_ref[:, cs], preferred_element_type=F32)
        prev = carry_ref[k] if sub == 0 else ends.pop((sub - 1, k))
        if sub == n_sub - 1:
            carry_ref[k] = a[tm - 2 * SUBLANES:tm]
        else:
            ends[(sub, k)] = a[tm - 2 * SUBLANES:tm]
        wrapped = [pltpu.roll(jnp.where(last_sublane, prev[g * SUBLANES:(g + 1) * SUBLANES],
                                        a[tm - (2 - g) * SUBLANES:tm - (1 - g) * SUBLANES]), 1, 0)
                   for g in range(2)]
        a1 = jnp.concatenate([wrapped[1], a[:tm - SUBLANES]], axis=0)
        a2 = jnp.concatenate([wrapped[0], wrapped[1], a[:tm - 2 * SUBLANES]], axis=0)
        return (a2 * cw_ref[0:1, cs] + a1 * cw_ref[1:2, cs] + a * cw_ref[2:3, cs]) + cb_ref[:, cs]

    f = None
    first = 0
    for group, hmid_ref in zip(FFN_GROUPS, hmids):
        for i in range(group):
            c = first + i
            gate = conv(c * FFN_CHUNK, 2 * c)
            val = conv(D_FF + c * FFN_CHUNK, 2 * c + 1)
            hmid_ref[:, i * FFN_CHUNK:(i + 1) * FFN_CHUNK] = (
                jax.nn.gelu(gate, approximate=True) * val).astype(BF16)
            if i == group - 1:
                ks = slice(first * FFN_CHUNK, (first + group) * FFN_CHUNK)
                part = jnp.dot(hmid_ref[...], wdn_ref[ks, :], preferred_element_type=F32)
                f = part if f is None else f + part
            yield
        first += group
    _store_token_rows(h + _rms(f, g3_ref[...]), pout_ref, out_ref, sub * tm)


def _layer_tail(mix_inputs, w_o_all, o_idx, h3, layer, g_all, w_up, cw, cb, w_down):
    batch, seq, _ = h3.shape
    n_sub = TAIL_SUBTILES
    tm = ROW_TILE // n_sub
    blk = ROW_TILE
    row = pl.BlockSpec((None, blk, D_MODEL), lambda b, t: (b, t, 0))
    gain = lambda i: _resident((None, 1, D_MODEL), lambda b, t: (4 * layer + i, 0, 0))
    if len(mix_inputs) == 1:
        mix_specs = [pl.BlockSpec((None, blk, B_WIDTH), lambda b, t: (b, t, 0))]
        extra_scratch = []
    else:
        mix_specs = 2 * [pl.BlockSpec((None, r, blk // r, GROUP_WIDTH), lambda b, t: (b, 0, t, 0))
                         for _, r in A_GROUPS]
        extra_scratch = 2 * [pltpu.VMEM((A_WIDTH // LANES, tm, LANES), F32)]
    return pl.pallas_call(
        functools.partial(_tail_kernel, tm=tm, n_mix=len(mix_inputs), n_sub=n_sub),
        grid=(batch, seq // blk),
        in_specs=mix_specs + [_resident((None,) + w_o_all.shape[1:], lambda b, t: (o_idx, 0, 0)),
                              row, gain(1), gain(2),
                              _resident((None, D_MODEL, 2 * D_FF), lambda b, t: (layer, 0, 0)),
                              _resident((None, CONV_W, 2 * D_FF), lambda b, t: (layer, 0, 0)),
                              _resident((None, 1, 2 * D_FF), lambda b, t: (layer, 0, 0)),
                              _resident((None, D_FF, D_MODEL), lambda b, t: (layer, 0, 0)),
                              gain(3)],
        out_specs=row,
        out_shape=jax.ShapeDtypeStruct(h3.shape, F32),
        scratch_shapes=([pltpu.VMEM((2 * D_FF // FFN_CHUNK, 2 * SUBLANES, FFN_CHUNK), F32)]
                        + n_sub * ([pltpu.VMEM((tm, group * FFN_CHUNK), BF16) for group in FFN_GROUPS]
                                   + [pltpu.VMEM((D_MODEL // LANES, tm + SUBLANES * SUBLANES, LANES), F32),
                                      pltpu.VMEM((D_MODEL // LANES, tm, LANES), F32)])
                        + extra_scratch),
        compiler_params=_params(2),
        name="layer_tail",
    )(*mix_inputs, w_o_all, h3, g_all, g_all, w_up, cw, cb, w_down, g_all)


def _split_bf16(c):
    pieces, rest = [], c
    for _ in range(BIAS_PIECES):
        piece = rest.astype(BF16)
        pieces.append(piece)
        rest = rest - piece.astype(F32)
    return jnp.concatenate(pieces, axis=1)


def _store_augmented(project, scale, c_tile, place_ref, ones_ref, out_ref):
    tm = c_tile.shape[0]
    slab = jnp.dot(_split_bf16(c_tile * LOG2E), place_ref[...], preferred_element_type=F32) + ones_ref[...]
    lane = lax.broadcasted_iota(jnp.int32, (tm, LANES), 1)
    own = lane < HEAD_DIM
    bias_lanes = lane < HEAD_DIM + 2 * BIAS_PIECES
    heads_per_piece = MXU_DIM // HEAD_DIM
    for h in range(B_HEADS):
        if h % heads_per_piece == 0:
            x = project(slice(h * HEAD_DIM, h * HEAD_DIM + MXU_DIM)) * scale
        hp = h % heads_per_piece
        pair = x[:, (hp // 2) * LANES:(hp // 2 + 1) * LANES]
        if h % 2 == 1:
            pair = pltpu.roll(pair, HEAD_DIM, 1)
        shift = (HEAD_DIM - BIAS_STRIDE * h) % LANES
        bias = pltpu.roll(slab, shift, 1) if shift else slab
        out_ref[:, h * LANES:(h + 1) * LANES] = jnp.where(
            own, pair, jnp.where(bias_lanes, bias, 0.0)).astype(BF16)


def _kvf_kernel(x_ref, g_ref, wkv_ref, wf_ref, bf_ref, place_ref, ones_ref, k_ref, v_ref, c_ref,
                carry_ref, *, tm):
    @pl.when(pl.program_id(1) == 0)
    def _():
        carry_ref[...] = jnp.zeros_like(carry_ref)

    xn = _rms(x_ref[...], g_ref[...]).astype(BF16)
    z = jnp.dot(xn, wf_ref[...], preferred_element_type=F32) + bf_ref[...]
    log_f = -(jnp.maximum(-z, 0.0) + jnp.log1p(jnp.exp(-jnp.abs(z))))
    rows = lax.broadcasted_iota(jnp.int32, (tm, LANES), 0)
    acc = log_f
    shift = 1
    while shift < tm:
        acc = acc + jnp.where(rows >= shift, pltpu.roll(acc, shift, 0), 0.0)
        shift *= 2
    acc = acc + carry_ref[...]
    c_ref[...] = acc
    carry_ref[...] = acc[tm - 1:tm, :]
    _store_augmented(lambda cols: jnp.dot(xn, wkv_ref[:, cols], preferred_element_type=F32), 1.0,
                     acc, place_ref, ones_ref, k_ref)
    pad_rows = lax.broadcasted_iota(jnp.int32, (VT_ROWS - HEAD_DIM, tm), 0)
    ones_pad = jnp.where(pad_rows == 0, 1.0, 0.0)
    heads_per_piece = MXU_DIM // HEAD_DIM
    for h0 in range(0, B_HEADS, heads_per_piece):
        cols = slice(B_WIDTH + h0 * HEAD_DIM, B_WIDTH + h0 * HEAD_DIM + MXU_DIM)
        vt = jnp.dot(xn, wkv_ref[:, cols], preferred_element_type=F32).T
        for hp in range(heads_per_piece):
            h = h0 + hp
            v_ref[h * VT_ROWS:(h + 1) * VT_ROWS, :] = jnp.concatenate(
                [vt[hp * HEAD_DIM:(hp + 1) * HEAD_DIM], ones_pad], axis=0).astype(BF16)


def _kvf_proj(h3, g, wkv, wf, bf, place_k, ones_k):
    batch, seq, _ = h3.shape
    tm = ROW_TILE
    row = lambda width: pl.BlockSpec((None, tm, width), lambda b, t: (b, t, 0))
    return pl.pallas_call(
        functools.partial(_kvf_kernel, tm=tm),
        grid=(batch, seq // tm),
        in_specs=[row(D_MODEL),
                  _resident((1, D_MODEL), lambda b, t: (0, 0)),
                  _resident((D_MODEL, 2 * B_WIDTH), lambda b, t: (0, 0)),
                  _resident((D_MODEL, LANES), lambda b, t: (0, 0)),
                  _resident((1, LANES), lambda b, t: (0, 0)),
                  _resident((BIAS_PIECES * LANES, LANES), lambda b, t: (0, 0)),
                  _resident((1, LANES), lambda b, t: (0, 0))],
        out_specs=[row(AUG_WIDTH),
                   pl.BlockSpec((None, None, B_HEADS * VT_ROWS, tm), lambda b, t: (b, t, 0, 0)),
                   row(LANES)],
        out_shape=[jax.ShapeDtypeStruct((batch, seq, AUG_WIDTH), BF16),
                   jax.ShapeDtypeStruct((batch, seq // tm, B_HEADS * VT_ROWS, tm), BF16),
                   jax.ShapeDtypeStruct((batch, seq, LANES), F32)],
        scratch_shapes=[pltpu.VMEM((1, LANES), F32)],
        compiler_params=_params(2),
        name="kvf_proj",
    )(h3, g, wkv, wf, bf, place_k, ones_k)


def _qproj_kernel(x_ref, g_ref, w_ref, c_ref, place_ref, ones_ref, o_ref):
    xn = _rms(x_ref[...], g_ref[...]).astype(BF16)
    _store_augmented(lambda cols: jnp.dot(xn, w_ref[:, cols], preferred_element_type=F32), SCALE * LOG2E,
                     c_ref[...], place_ref, ones_ref, o_ref)


def _q_proj(h2, g, w_all, idx, c2, place_q, ones_q):
    m = h2.shape[0]
    tm = ROW_TILE
    return pl.pallas_call(
        _qproj_kernel,
        grid=(m // tm,),
        in_specs=[pl.BlockSpec((tm, D_MODEL), lambda i: (i, 0)),
                  _resident((1, D_MODEL), lambda i: (0, 0)),
                  _resident((None, D_MODEL, B_WIDTH), lambda i: (idx, 0, 0)),
                  pl.BlockSpec((tm, LANES), lambda i: (i, 0)),
                  _resident((BIAS_PIECES * LANES, LANES), lambda i: (0, 0)),
                  _resident((1, LANES), lambda i: (0, 0))],
        out_specs=pl.BlockSpec((tm, AUG_WIDTH), lambda i: (i, 0)),
        out_shape=jax.ShapeDtypeStruct((m, AUG_WIDTH), BF16),
        compiler_params=_params(1),
        name="q_proj",
    )(h2, g, w_all, c2, place_q, ones_q)


def _fox_kernel(q_ref, k_ref, vt_ref, o_ref, *, tq, nh):
    half = tq // 2

    def scores_of(qi, j, key0, nkeys, query0, diagonal):
        return [lax.dot_general(k_ref[j * tq + key0:j * tq + key0 + nkeys, hh * LANES:(hh + 1) * LANES],
                                q_ref[qi * tq + query0:(qi + 1) * tq, hh * LANES:(hh + 1) * LANES],
                                NT_DIMS, preferred_element_type=F32) for hh in range(nh)]

    def update(step, scores, state):
        _, j, key0, nkeys, query0, diagonal = step
        stats = []
        for hh in range(nh):
            m = state[hh][0][:, query0:]
            s = scores[hh]
            if diagonal:
                visible = (lax.broadcasted_iota(jnp.int32, s.shape, 0)
                           <= lax.broadcasted_iota(jnp.int32, s.shape, 1))
                s = jnp.where(visible, s, NEG)
            m_new = jnp.maximum(m, jnp.max(s, axis=0, keepdims=True))
            stats.append((m_new, jnp.exp2(m - m_new), jnp.exp2(s - m_new).astype(BF16)))
        out = []
        for hh in range(nh):
            m_new, alpha, p = stats[hh]
            vt = vt_ref[j, hh * VT_ROWS:(hh + 1) * VT_ROWS, key0:key0 + nkeys]
            m_old, acc_old = state[hh]
            acc = alpha * acc_old[:, query0:] + jnp.dot(vt, p, preferred_element_type=F32)
            if query0:
                m_new = jnp.concatenate([m_old[:, :query0], m_new], axis=1)
                acc = jnp.concatenate([acc_old[:, :query0], acc], axis=1)
            out.append((m_new, acc))
        return tuple(out)

    steps = []
    for qi in range(q_ref.shape[0] // tq):
        steps += [(qi, j, 0, tq, 0, False) for j in range(qi)]
        steps += [(qi, qi, 0, half, 0, True), (qi, qi, half, half, half, True)]
    scores = scores_of(*steps[0])
    state = None
    for n, step in enumerate(steps):
        nxt = scores_of(*steps[n + 1]) if n + 1 < len(steps) else None
        qi, j, key0 = step[:3]
        if j == 0 and key0 == 0:
            state = tuple((jnp.full((1, tq), NEG, F32), jnp.zeros((VT_ROWS, tq), F32)) for _ in range(nh))
        state = update(step, scores, state)
        if j == qi and key0:
            out_t = jnp.concatenate([acc[:HEAD_DIM] / acc[HEAD_DIM:HEAD_DIM + 1] for _, acc in state], axis=0)
            o_ref[qi * tq:(qi + 1) * tq, :] = out_t.T.astype(BF16)
        scores = nxt


def _fox_attention(q_aug, k_aug, v_t):
    batch, nt, _, tq = v_t.shape
    assert tq == FOX_TILE
    seq = nt * tq
    nh = FOX_HEADS
    q3 = q_aug.reshape(batch, seq, AUG_WIDTH)
    return pl.pallas_call(
        functools.partial(_fox_kernel, tq=tq, nh=nh),
        grid=(batch, B_HEADS // nh),
        in_specs=[pl.BlockSpec((None, seq, nh * LANES), lambda b, hp: (b, 0, hp)),
                  pl.BlockSpec((None, seq, nh * LANES), lambda b, hp: (b, 0, hp)),
                  pl.BlockSpec((None, nt, nh * VT_ROWS, tq), lambda b, hp: (b, 0, hp, 0))],
        out_specs=pl.BlockSpec((None, seq, nh * HEAD_DIM), lambda b, hp: (b, 0, hp)),
        out_shape=jax.ShapeDtypeStruct((batch, seq, B_WIDTH), BF16),
        compiler_params=_params(2),
        name="fox_attn",
    )(q3, k_aug, v_t)


def _bias_placement():
    assert 2 * BIAS_PIECES <= BIAS_STRIDE
    place_q = np.zeros((BIAS_PIECES * LANES, LANES), np.float32)
    place_k = np.zeros((BIAS_PIECES * LANES, LANES), np.float32)
    ones_q = np.zeros((1, LANES), np.float32)
    ones_k = np.zeros((1, LANES), np.float32)
    for h in range(B_HEADS):
        base = h * BIAS_STRIDE
        for piece in range(BIAS_PIECES):
            place_q[piece * LANES + h, base + piece] = 1.0
            ones_k[0, base + piece] = 1.0
            place_k[piece * LANES + h, base + BIAS_PIECES + piece] = -1.0
            ones_q[0, base + BIAS_PIECES + piece] = 1.0
    return (jnp.asarray(place_q, BF16), jnp.asarray(ones_q), jnp.asarray(place_k, BF16), jnp.asarray(ones_k))


def _rope_lane_tables(seq):
    pos = jnp.arange(seq, dtype=F32)
    inv = ROPE_THETA ** (-jnp.arange(0, ROPE_DIM, 2, dtype=F32) / ROPE_DIM)
    ang = pos[:, None] * inv[None, :]
    cos, sin = jnp.cos(ang), jnp.sin(ang)
    ones = jnp.ones((seq, HEAD_DIM - ROPE_DIM), F32)
    zeros = jnp.zeros((seq, HEAD_DIM - ROPE_DIM), F32)
    zh = jnp.zeros((seq, ROPE_HALF), F32)
    cos_h = jnp.concatenate([cos, cos, ones], axis=1)
    sa_h = jnp.concatenate([zh, sin, zeros], axis=1)
    sb_h = jnp.concatenate([-sin, zh, zeros], axis=1)
    rep = LANES // HEAD_DIM
    return tuple(jnp.tile(t, (1, rep)) for t in (cos_h, sa_h, sb_h))


def kernel(x, norm_gains, w_qkv_a, w_o_a, w_q_b, w_o_b, kv_norm, w_kvf, b_f, w_up, conv_w, conv_b, w_down):
    batch, seq, _ = x.shape
    m = batch * seq
    cos_t, sa_t, sb_t = _rope_lane_tables(seq)
    place_q, ones_q, place_k, ones_k = _bias_placement()
    gains = norm_gains.reshape(DEPTH * 4, 1, D_MODEL)
    w_qkv_a, w_o_a, w_q_b, w_o_b, w_up, w_down = (
        t.astype(BF16) for t in (w_qkv_a, w_o_a, w_q_b, w_o_b, w_up, w_down))
    w_kv = w_kvf[:, :2 * B_WIDTH].astype(BF16)
    w_f = jnp.pad(w_kvf[:, 2 * B_WIDTH:], ((0, 0), (0, LANES - B_HEADS))).astype(BF16)
    b_f = jnp.pad(b_f, (0, LANES - B_HEADS)).reshape(1, LANES)
    conv_b = conv_b.reshape(DEPTH, 1, 2 * D_FF)

    h = x
    k_aug = v_sh = c_sh = None
    for l in range(DEPTH):
        g = lambda i: gains[4 * l + i]
        if l < N_A:
            qkv = _qkv_proj(h, g(0), w_qkv_a, l, cos_t, sa_t, sb_t)
            os_, ls_ = zip(*[_dilated_attention(qkv[gi], gi, window, r)
                             for gi, (window, r) in enumerate(A_GROUPS)])
            mix_inputs, w_o, o_idx = [*os_, *ls_], w_o_a, l
        else:
            if l == N_A:
                k_aug, v_sh, c_sh = _kvf_proj(h, kv_norm.reshape(1, D_MODEL), w_kv, w_f, b_f,
                                              place_k, ones_k)
            j = l - N_A
            q_aug = _q_proj(h.reshape(m, D_MODEL), g(0), w_q_b, j, c_sh.reshape(m, LANES),
                            place_q, ones_q)
            mix_inputs, w_o, o_idx = [_fox_attention(q_aug, k_aug, v_sh)], w_o_b, j
        h = _layer_tail(mix_inputs, w_o, o_idx, h, l, gains, w_up, conv_w, conv_b, w_down)
    return h
```

```python
import functools

import numpy as np

import jax
import jax.numpy as jnp
from jax import lax
from jax.experimental import pallas as pl
from jax.experimental.pallas import tpu as pltpu

D_MODEL = 1024
DEPTH = 4
HEAD_DIM = 64
A_GROUPS = ((128, 1), (512, 4), (2048, 16))
A_HEADS_PER_GROUP = 4
A_HEADS = A_HEADS_PER_GROUP * len(A_GROUPS)
A_WIDTH = A_HEADS * HEAD_DIM
GROUP_WIDTH = A_HEADS_PER_GROUP * HEAD_DIM
B_HEADS = D_MODEL // HEAD_DIM
B_WIDTH = B_HEADS * HEAD_DIM
N_A = DEPTH // 2
D_FF = 2816
CONV_W = 3
ROPE_DIM = HEAD_DIM // 4
ROPE_HALF = ROPE_DIM // 2
ROPE_THETA = 500000.0
BLK = 128
EPS = 1e-6
NEG = -1e30
SCALE = HEAD_DIM ** -0.5
LOG2E = float(np.log2(np.e))

LANES = 128
SUBLANES = 8
MXU_DIM = 256
VMEM_LIMIT = 56 * 1024 * 1024

ROW_TILE = 512
FFN_CHUNK = MXU_DIM
TAIL_SUBTILES = 2
FFN_GROUPS = (11,)
FOX_TILE = 512
VT_ROWS = HEAD_DIM + 16
FOX_HEADS = 4
AUG_WIDTH = B_HEADS * LANES
BIAS_PIECES = 3
BIAS_STRIDE = LANES // B_HEADS
DILATED_UNROLL = (3, 4, 4)

F32 = jnp.float32
BF16 = jnp.bfloat16
NT_DIMS = (((1,), (1,)), ((), ()))


def _params(n_axes):
    return pltpu.CompilerParams(dimension_semantics=("arbitrary",) * n_axes,
                                vmem_limit_bytes=VMEM_LIMIT)


def _resident(shape, index_map):
    return pl.BlockSpec(shape, index_map, pipeline_mode=pl.Buffered(1))


def _rms(x, g):
    return x * lax.rsqrt(jnp.mean(x * x, axis=-1, keepdims=True) + EPS) * g


def _qkv_kernel(x_ref, g_ref, w_ref, cos_ref, sa_ref, sb_ref, o0_ref, o1_ref, o2_ref, ybuf_ref, *, tm):
    xn = _rms(x_ref[...], g_ref[...]).astype(BF16)
    y = jnp.dot(xn, w_ref[...], preferred_element_type=F32)
    cos, sa, sb = cos_ref[...], sa_ref[...], sb_ref[...]
    slabs_per_group = 3 * GROUP_WIDTH // LANES
    slabs_per_kind = GROUP_WIDTH // LANES
    for c in range(3 * A_WIDTH // LANES):
        grp, kind = c // slabs_per_group, (c % slabs_per_group) // slabs_per_kind
        src = kind * A_WIDTH + grp * GROUP_WIDTH + (c % slabs_per_kind) * LANES
        t = y[:, src:src + LANES]
        if kind < 2:
            t = t * cos + pltpu.roll(t, ROPE_HALF, 1) * sa + pltpu.roll(t, LANES - ROPE_HALF, 1) * sb
        if kind == 0:
            t = t * SCALE
        if grp == 0:
            o0_ref[0, :, c * LANES:(c + 1) * LANES] = t.astype(BF16)
        else:
            ybuf_ref[c - slabs_per_group] = t
    for grp, o_ref in ((1, o1_ref), (2, o2_ref)):
        r = A_GROUPS[grp][1]
        for j in range(r):
            for c in range(slabs_per_group):
                o_ref[j, :, c * LANES:(c + 1) * LANES] = ybuf_ref[
                    (grp - 1) * slabs_per_group + c, pl.ds(j, tm // r, stride=r), :].astype(BF16)


def _qkv_proj(h3, g, w_all, idx, cos_t, sa_t, sb_t):
    batch, seq, _ = h3.shape
    tm = ROW_TILE
    n = 3 * A_WIDTH
    gw = 3 * GROUP_WIDTH
    tab = pl.BlockSpec((tm, LANES), lambda b, t: (t, 0))
    (_, r1), (_, r2) = A_GROUPS[1], A_GROUPS[2]
    return pl.pallas_call(
        functools.partial(_qkv_kernel, tm=tm),
        grid=(batch, seq // tm),
        in_specs=[pl.BlockSpec((None, tm, D_MODEL), lambda b, t: (b, t, 0)),
                  _resident((1, D_MODEL), lambda b, t: (0, 0)),
                  _resident((None, D_MODEL, n), lambda b, t: (idx, 0, 0)),
                  tab, tab, tab],
        out_specs=[pl.BlockSpec((None, 1, tm, gw), lambda b, t: (b, 0, t, 0)),
                   pl.BlockSpec((None, r1, tm // r1, gw), lambda b, t: (b, 0, t, 0)),
                   pl.BlockSpec((None, r2, tm // r2, gw), lambda b, t: (b, 0, t, 0))],
        out_shape=[jax.ShapeDtypeStruct((batch, 1, seq, gw), BF16),
                   jax.ShapeDtypeStruct((batch, r1, seq // r1, gw), BF16),
                   jax.ShapeDtypeStruct((batch, r2, seq // r2, gw), BF16)],
        scratch_shapes=[pltpu.VMEM((2 * gw // LANES, tm, LANES), F32)],
        compiler_params=_params(2),
        name="qkv_proj",
    )(h3, g, w_all, cos_t, sa_t, sb_t)


def _dilated_kernel(*refs, plans):
    n = len(plans)
    calls = []
    for g, plan in enumerate(plans):
        q_ref, k_ref, v_ref = refs[3 * g:3 * g + 3]
        o_ref, lse_ref = refs[3 * n + 2 * g:3 * n + 2 * g + 2]
        calls.append(_dilated_calls(q_ref, k_ref, v_ref, o_ref, lse_ref, *plan))
    for i in range(max(len(c) for c in calls)):
        for group_calls in calls:
            if i < len(group_calls):
                group_calls[i]()


def _dilated_calls(q_ref, k_ref, v_ref, o_ref, lse_ref, n_res, nb, n_back, unroll):
    nh = A_HEADS_PER_GROUP
    lane = lax.broadcasted_iota(jnp.int32, (BLK, GROUP_WIDTH), 1)
    head_masks = [(lane >= h * HEAD_DIM) & (lane < (h + 1) * HEAD_DIM) for h in range(nh)]

    def band(width):
        qi = lax.broadcasted_iota(jnp.int32, (nh * BLK, width), 0) & (BLK - 1)
        kj = lax.broadcasted_iota(jnp.int32, (nh * BLK, width), 1)
        rel = qi + (width - BLK) - kj
        return (rel >= 0) & (rel <= n_back)

    def attend(items, width):
        mask = band(width)
        scores = []
        for res, q0, k0 in items:
            q = q_ref[res, pl.ds(q0, BLK), :]
            qs = jnp.concatenate([jnp.where(hm, q, jnp.zeros_like(q)) for hm in head_masks], axis=0)
            scores.append(lax.dot_general(qs, k_ref[res, pl.ds(k0, width), :], NT_DIMS,
                                          preferred_element_type=F32))
        probs = []
        for s in scores:
            s = jnp.where(mask, s, NEG)
            m = jnp.max(s, axis=-1, keepdims=True)
            p = jnp.exp(s - m)
            l = jnp.sum(p, axis=-1, keepdims=True)
            probs.append(((p / l).astype(BF16), m + jnp.log(l)))
        for (res, q0, k0), (pn, lse) in zip(items, probs):
            pv = jnp.dot(pn, v_ref[res, pl.ds(k0, width), :], preferred_element_type=F32)
            o = pv[0:BLK]
            ls = jnp.broadcast_to(lse[0:BLK], (BLK, GROUP_WIDTH))
            for h in range(1, nh):
                o = jnp.where(head_masks[h], pv[h * BLK:(h + 1) * BLK], o)
                ls = jnp.where(head_masks[h], lse[h * BLK:(h + 1) * BLK], ls)
            o_ref[res, pl.ds(q0, BLK), :] = o.astype(BF16)
            lse_ref[res, pl.ds(q0, BLK), :] = ls

    firsts = [(res, 0, 0) for res in range(n_res)]
    rest = [(res, i * BLK, (i - 1) * BLK) for i in range(1, nb) for res in range(n_res)]
    return ([functools.partial(attend, firsts[i:i + unroll], BLK) for i in range(0, n_res, unroll)]
            + [functools.partial(attend, rest[i:i + unroll], 2 * BLK) for i in range(0, len(rest), unroll)])


def _dilated_attention(qkvs):
    batch = qkvs[0].shape[0]
    plans, in_specs, out_specs, out_shapes = [], [], [], []
    for (window, r), unroll, qkv_g in zip(A_GROUPS, DILATED_UNROLL, qkvs):
        ln = qkv_g.shape[2]
        assert ln % BLK == 0
        plans.append((r, ln // BLK, window // r, unroll))
        shape = (None, r, ln, GROUP_WIDTH)
        in_specs += [pl.BlockSpec(shape, functools.partial(lambda b, which: (b, 0, 0, which), which=which))
                     for which in range(3)]
        out_specs.append(pl.BlockSpec(shape, lambda b: (b, 0, 0, 0)))
        out_shapes.append((batch, r, ln, GROUP_WIDTH))
    outs = pl.pallas_call(
        functools.partial(_dilated_kernel, plans=tuple(plans)),
        grid=(batch,),
        in_specs=in_specs,
        out_specs=[s for s in out_specs for _ in range(2)],
        out_shape=[jax.ShapeDtypeStruct(s, dt) for s in out_shapes for dt in (BF16, F32)],
        compiler_params=_params(1),
        name="dilated_attn",
    )(*[qkv_g for qkv_g in qkvs for _ in range(3)])
    return list(outs[0::2]) + list(outs[1::2])


def _mix_dilated(o0_ref, o1_ref, o2_ref, l0_ref, l1_ref, l2_ref, w_ref, obuf_ref, lbuf_ref, *, tm, sub):
    slabs = GROUP_WIDTH // LANES
    for grp, (o_ref, l_ref) in enumerate(((o0_ref, l0_ref), (o1_ref, l1_ref), (o2_ref, l2_ref))):
        r = A_GROUPS[grp][1]
        src = slice(sub * tm // r, (sub + 1) * tm // r)
        for j in range(r):
            rows = pl.ds(j, tm // r, stride=r) if r > 1 else slice(None)
            for c in range(slabs):
                cols = slice(c * LANES, (c + 1) * LANES)
                obuf_ref[grp * slabs + c, rows, :] = o_ref[j, src, cols].astype(F32)
                lbuf_ref[grp * slabs + c, rows, :] = l_ref[j, src, cols]

    def group(buf_ref, grp):
        return jnp.concatenate([buf_ref[grp * slabs + c] for c in range(slabs)], axis=1)

    l0, l1, l2 = (group(lbuf_ref, grp) for grp in range(3))
    mx = jnp.maximum(jnp.maximum(l0, l1), l2)
    e0, e1, e2 = jnp.exp(l0 - mx), jnp.exp(l1 - mx), jnp.exp(l2 - mx)
    den = e0 + e1 + e2
    mixed = jnp.concatenate([(group(obuf_ref, grp) * (e / den)).astype(BF16)
                             for grp, e in enumerate((e0, e1, e2))], axis=1)
    return jnp.dot(mixed, w_ref[...], preferred_element_type=F32)


def _to_strided_rows(x, buf_ref):
    tm, d = x.shape
    n = tm // SUBLANES
    pitch = buf_ref.shape[1] // SUBLANES
    slabs = d // LANES
    for c in range(slabs):
        for s in range(SUBLANES):
            buf_ref[c, s * pitch:s * pitch + n, :] = x[s * n:(s + 1) * n, c * LANES:(c + 1) * LANES]
    return jnp.concatenate(
        [jnp.concatenate([buf_ref[c, pl.ds(i, SUBLANES, stride=pitch), :] for c in range(slabs)], axis=1)
         for i in range(n)], axis=0)


def _store_token_rows(y, buf_ref, out_ref, row0):
    tm, d = y.shape
    n = tm // SUBLANES
    per = n // SUBLANES
    slabs = d // LANES
    for c in range(slabs):
        buf_ref[c] = y[:, c * LANES:(c + 1) * LANES]
    for j in range(n):
        start = SUBLANES * SUBLANES * (j % per) + j // per
        out_ref[row0 + j * SUBLANES:row0 + (j + 1) * SUBLANES, :] = jnp.concatenate(
            [buf_ref[c, pl.ds(start, SUBLANES, stride=SUBLANES), :] for c in range(slabs)], axis=1)


def _tail_kernel(*refs, tm, n_mix, n_sub):
    mix_refs, refs = refs[:n_mix + 1], refs[n_mix + 1:]
    h_ref, g1_ref, g2_ref, wup_ref, cw_ref, cb_ref, wdn_ref, g3_ref, out_ref, carry_ref = refs[:10]
    scratch = refs[10:]
    per_sub = len(FFN_GROUPS) + 2
    extra = scratch[n_sub * per_sub:]

    @pl.when(pl.program_id(1) == 0)
    def _():
        carry_ref[...] = jnp.zeros_like(carry_ref)

    ends = {}
    subs = [_tail_sub(mix_refs, extra, h_ref, g1_ref, g2_ref, wup_ref, cw_ref, cb_ref, wdn_ref, g3_ref,
                      out_ref, carry_ref, scratch[i * per_sub:(i + 1) * per_sub], ends,
                      tm=tm, sub=i, n_sub=n_sub) for i in range(n_sub)]
    live = []
    while subs or live:
        if subs:
            live.append(subs.pop(0))
        for gen in list(live):
            if next(gen, "done") == "done":
                live.remove(gen)


def _tail_sub(mix_refs, extra, h_ref, g1_ref, g2_ref, wup_ref, cw_ref, cb_ref, wdn_ref, g3_ref, out_ref,
              carry_ref, own, ends, *, tm, sub, n_sub):
    hmids, pin_ref, pout_ref = own[:len(FFN_GROUPS)], own[-2], own[-1]
    rows = slice(sub * tm, (sub + 1) * tm)
    if len(mix_refs) == 2:
        o_ref, w_ref = mix_refs
        mix = jnp.dot(o_ref[rows, :], w_ref[...], preferred_element_type=F32)
    else:
        mix = _mix_dilated(*mix_refs, *extra, tm=tm, sub=sub)
    h = _to_strided_rows(h_ref[rows, :] + _rms(mix, g1_ref[...]), pin_ref)
    xn = _rms(h, g2_ref[...]).astype(BF16)
    last_sublane = lax.broadcasted_iota(jnp.int32, (SUBLANES, FFN_CHUNK), 0) == SUBLANES - 1
    yield

    def conv(col, k):
        cs = slice(col, col + FFN_CHUNK)
        a = jnp.dot(xn, wup_ref[:, cs], preferred_element_type=F32)
        prev = carry_ref[k] if sub == 0 else ends.pop((sub - 1, k))
        if sub == n_sub - 1:
            carry_ref[k] = a[tm - 2 * SUBLANES:tm]
        else:
            ends[(sub, k)] = a[tm - 2 * SUBLANES:tm]
        wrapped = [pltpu.roll(jnp.where(last_sublane, prev[g * SUBLANES:(g + 1) * SUBLANES],
                                        a[tm - (2 - g) * SUBLANES:tm - (1 - g) * SUBLANES]), 1, 0)
                   for g in range(2)]
        a1 = jnp.concatenate([wrapped[1], a[:tm - SUBLANES]], axis=0)
        a2 = jnp.concatenate([wrapped[0], wrapped[1], a[:tm - 2 * SUBLANES]], axis=0)
        return (a2 * cw_ref[0:1, cs] + a1 * cw_ref[1:2, cs] + a * cw_ref[2:3, cs]) + cb_ref[:, cs]

    f = None
    first = 0
    for group, hmid_ref in zip(FFN_GROUPS, hmids):
        for i in range(group):
            c = first + i
            gate = conv(c * FFN_CHUNK, 2 * c)
            val = conv(D_FF + c * FFN_CHUNK, 2 * c + 1)
            hmid_ref[:, i * FFN_CHUNK:(i + 1) * FFN_CHUNK] = (
                jax.nn.gelu(gate, approximate=True) * val).astype(BF16)
            if i == group - 1:
                ks = slice(first * FFN_CHUNK, (first + group) * FFN_CHUNK)
                part = jnp.dot(hmid_ref[...], wdn_ref[ks, :], preferred_element_type=F32)
                f = part if f is None else f + part
            yield
        first += group
    _store_token_rows(h + _rms(f, g3_ref[...]), pout_ref, out_ref, sub * tm)


def _layer_tail(mix_inputs, w_o_all, o_idx, h3, layer, g_all, w_up, cw, cb, w_down):
    batch, seq, _ = h3.shape
    n_sub = TAIL_SUBTILES
    tm = ROW_TILE // n_sub
    blk = ROW_TILE
    row = pl.BlockSpec((None, blk, D_MODEL), lambda b, t: (b, t, 0))
    gain = lambda i: _resident((None, 1, D_MODEL), lambda b, t: (4 * layer + i, 0, 0))
    if len(mix_inputs) == 1:
        mix_specs = [pl.BlockSpec((None, blk, B_WIDTH), lambda b, t: (b, t, 0))]
        extra_scratch = []
    else:
        mix_specs = 2 * [pl.BlockSpec((None, r, blk // r, GROUP_WIDTH), lambda b, t: (b, 0, t, 0))
                         for _, r in A_GROUPS]
        extra_scratch = 2 * [pltpu.VMEM((A_WIDTH // LANES, tm, LANES), F32)]
    return pl.pallas_call(
        functools.partial(_tail_kernel, tm=tm, n_mix=len(mix_inputs), n_sub=n_sub),
        grid=(batch, seq // blk),
        in_specs=mix_specs + [_resident((None,) + w_o_all.shape[1:], lambda b, t: (o_idx, 0, 0)),
                              row, gain(1), gain(2),
                              _resident((None, D_MODEL, 2 * D_FF), lambda b, t: (layer, 0, 0)),
                              _resident((None, CONV_W, 2 * D_FF), lambda b, t: (layer, 0, 0)),
                              _resident((None, 1, 2 * D_FF), lambda b, t: (layer, 0, 0)),
                              _resident((None, D_FF, D_MODEL), lambda b, t: (layer, 0, 0)),
                              gain(3)],
        out_specs=row,
        out_shape=jax.ShapeDtypeStruct(h3.shape, F32),
        scratch_shapes=([pltpu.VMEM((2 * D_FF // FFN_CHUNK, 2 * SUBLANES, FFN_CHUNK), F32)]
                        + n_sub * ([pltpu.VMEM((tm, group * FFN_CHUNK), BF16) for group in FFN_GROUPS]
                                   + [pltpu.VMEM((D_MODEL // LANES, tm + SUBLANES * SUBLANES, LANES), F32),
                                      pltpu.VMEM((D_MODEL // LANES, tm, LANES), F32)])
                        + extra_scratch),
        compiler_params=_params(2),
        name="layer_tail",
    )(*mix_inputs, w_o_all, h3, g_all, g_all, w_up, cw, cb, w_down, g_all)


def _split_bf16(c):
    pieces, rest = [], c
    for _ in range(BIAS_PIECES):
        piece = rest.astype(BF16)
        pieces.append(piece)
        rest = rest - piece.astype(F32)
    return jnp.concatenate(pieces, axis=1)


def _store_augmented(project, scale, c_tile, place_ref, ones_ref, out_ref):
    tm = c_tile.shape[0]
    slab = jnp.dot(_split_bf16(c_tile * LOG2E), place_ref[...], preferred_element_type=F32) + ones_ref[...]
    lane = lax.broadcasted_iota(jnp.int32, (tm, LANES), 1)
    own = lane < HEAD_DIM
    bias_lanes = lane < HEAD_DIM + 2 * BIAS_PIECES
    heads_per_piece = MXU_DIM // HEAD_DIM
    for h in range(B_HEADS):
        if h % heads_per_piece == 0:
            x = project(slice(h * HEAD_DIM, h * HEAD_DIM + MXU_DIM)) * scale
        hp = h % heads_per_piece
        pair = x[:, (hp // 2) * LANES:(hp // 2 + 1) * LANES]
        if h % 2 == 1:
            pair = pltpu.roll(pair, HEAD_DIM, 1)
        shift = (HEAD_DIM - BIAS_STRIDE * h) % LANES
        bias = pltpu.roll(slab, shift, 1) if shift else slab
        out_ref[:, h * LANES:(h + 1) * LANES] = jnp.where(
            own, pair, jnp.where(bias_lanes, bias, 0.0)).astype(BF16)


def _kvf_kernel(x_ref, g_ref, wkv_ref, wf_ref, bf_ref, place_ref, ones_ref, k_ref, v_ref, c_ref,
                carry_ref, *, tm):
    @pl.when(pl.program_id(1) == 0)
    def _():
        carry_ref[...] = jnp.zeros_like(carry_ref)

    xn = _rms(x_ref[...], g_ref[...]).astype(BF16)
    z = jnp.dot(xn, wf_ref[...], preferred_element_type=F32) + bf_ref[...]
    log_f = -(jnp.maximum(-z, 0.0) + jnp.log1p(jnp.exp(-jnp.abs(z))))
    rows = lax.broadcasted_iota(jnp.int32, (tm, LANES), 0)
    acc = log_f
    shift = 1
    while shift < tm:
        acc = acc + jnp.where(rows >= shift, pltpu.roll(acc, shift, 0), 0.0)
        shift *= 2
    acc = acc + carry_ref[...]
    c_ref[...] = acc
    carry_ref[...] = acc[tm - 1:tm, :]
    _store_augmented(lambda cols: jnp.dot(xn, wkv_ref[:, cols], preferred_element_type=F32), 1.0,
                     acc, place_ref, ones_ref, k_ref)
    pad_rows = lax.broadcasted_iota(jnp.int32, (VT_ROWS - HEAD_DIM, tm), 0)
    ones_pad = jnp.where(pad_rows == 0, 1.0, 0.0)
    heads_per_piece = MXU_DIM // HEAD_DIM
    for h0 in range(0, B_HEADS, heads_per_piece):
        cols = slice(B_WIDTH + h0 * HEAD_DIM, B_WIDTH + h0 * HEAD_DIM + MXU_DIM)
        vt = jnp.dot(xn, wkv_ref[:, cols], preferred_element_type=F32).T
        for hp in range(heads_per_piece):
            h = h0 + hp
            v_ref[h * VT_ROWS:(h + 1) * VT_ROWS, :] = jnp.concatenate(
                [vt[hp * HEAD_DIM:(hp + 1) * HEAD_DIM], ones_pad], axis=0).astype(BF16)


def _kvf_proj(h3, g, wkv, wf, bf, place_k, ones_k):
    batch, seq, _ = h3.shape
    tm = ROW_TILE
    row = lambda width: pl.BlockSpec((None, tm, width), lambda b, t: (b, t, 0))
    return pl.pallas_call(
        functools.partial(_kvf_kernel, tm=tm),
        grid=(batch, seq // tm),
        in_specs=[row(D_MODEL),
                  _resident((1, D_MODEL), lambda b, t: (0, 0)),
                  _resident((D_MODEL, 2 * B_WIDTH), lambda b, t: (0, 0)),
                  _resident((D_MODEL, LANES), lambda b, t: (0, 0)),
                  _resident((1, LANES), lambda b, t: (0, 0)),
                  _resident((BIAS_PIECES * LANES, LANES), lambda b, t: (0, 0)),
                  _resident((1, LANES), lambda b, t: (0, 0))],
        out_specs=[row(AUG_WIDTH),
                   pl.BlockSpec((None, None, B_HEADS * VT_ROWS, tm), lambda b, t: (b, t, 0, 0)),
                   row(LANES)],
        out_shape=[jax.ShapeDtypeStruct((batch, seq, AUG_WIDTH), BF16),
                   jax.ShapeDtypeStruct((batch, seq // tm, B_HEADS * VT_ROWS, tm), BF16),
                   jax.ShapeDtypeStruct((batch, seq, LANES), F32)],
        scratch_shapes=[pltpu.VMEM((1, LANES), F32)],
        compiler_params=_params(2),
        name="kvf_proj",
    )(h3, g, wkv, wf, bf, place_k, ones_k)


def _qproj_kernel(x_ref, g_ref, w_ref, c_ref, place_ref, ones_ref, o_ref):
    xn = _rms(x_ref[...], g_ref[...]).astype(BF16)
    _store_augmented(lambda cols: jnp.dot(xn, w_ref[:, cols], preferred_element_type=F32), SCALE * LOG2E,
                     c_ref[...], place_ref, ones_ref, o_ref)


def _q_proj(h2, g, w_all, idx, c2, place_q, ones_q):
    m = h2.shape[0]
    tm = ROW_TILE
    return pl.pallas_call(
        _qproj_kernel,
        grid=(m // tm,),
        in_specs=[pl.BlockSpec((tm, D_MODEL), lambda i: (i, 0)),
                  _resident((1, D_MODEL), lambda i: (0, 0)),
                  _resident((None, D_MODEL, B_WIDTH), lambda i: (idx, 0, 0)),
                  pl.BlockSpec((tm, LANES), lambda i: (i, 0)),
                  _resident((BIAS_PIECES * LANES, LANES), lambda i: (0, 0)),
                  _resident((1, LANES), lambda i: (0, 0))],
        out_specs=pl.BlockSpec((tm, AUG_WIDTH), lambda i: (i, 0)),
        out_shape=jax.ShapeDtypeStruct((m, AUG_WIDTH), BF16),
        compiler_params=_params(1),
        name="q_proj",
    )(h2, g, w_all, c2, place_q, ones_q)


def _fox_kernel(q_ref, k_ref, vt_ref, o_ref, *, tq, nh):
    half = tq // 2

    def scores_of(qi, j, key0, nkeys, query0, diagonal):
        return [lax.dot_general(k_ref[j * tq + key0:j * tq + key0 + nkeys, hh * LANES:(hh + 1) * LANES],
                                q_ref[qi * tq + query0:(qi + 1) * tq, hh * LANES:(hh + 1) * LANES],
                                NT_DIMS, preferred_element_type=F32) for hh in range(nh)]

    def update(step, scores, state):
        _, j, key0, nkeys, query0, diagonal = step
        stats = []
        for hh in range(nh):
            m = state[hh][0][:, query0:]
            s = scores[hh]
            if diagonal:
                visible = (lax.broadcasted_iota(jnp.int32, s.shape, 0)
                           <= lax.broadcasted_iota(jnp.int32, s.shape, 1))
                s = jnp.where(visible, s, NEG)
            m_new = jnp.maximum(m, jnp.max(s, axis=0, keepdims=True))
            stats.append((m_new, jnp.exp2(m - m_new), jnp.exp2(s - m_new).astype(BF16)))
        out = []
        for hh in range(nh):
            m_new, alpha, p = stats[hh]
            vt = vt_ref[j, hh * VT_ROWS:(hh + 1) * VT_ROWS, key0:key0 + nkeys]
            m_old, acc_old = state[hh]
            acc = alpha * acc_old[:, query0:] + jnp.dot(vt, p, preferred_element_type=F32)
            if query0:
                m_new = jnp.concatenate([m_old[:, :query0], m_new], axis=1)
                acc = jnp.concatenate([acc_old[:, :query0], acc], axis=1)
            out.append((m_new, acc))
        return tuple(out)

    steps = []
    for qi in range(q_ref.shape[0] // tq):
        steps += [(qi, j, 0, tq, 0, False) for j in range(qi)]
        steps += [(qi, qi, 0, half, 0, True), (qi, qi, half, half, half, True)]
    scores = scores_of(*steps[0])
    state = None
    for n, step in enumerate(steps):
        nxt = scores_of(*steps[n + 1]) if n + 1 < len(steps) else None
        qi, j, key0 = step[:3]
        if j == 0 and key0 == 0:
            state = tuple((jnp.full((1, tq), NEG, F32), jnp.zeros((VT_ROWS, tq), F32)) for _ in range(nh))
        state = update(step, scores, state)
        if j == qi and key0:
            out_t = jnp.concatenate([acc[:HEAD_DIM] / acc[HEAD_DIM:HEAD_DIM + 1] for _, acc in state], axis=0)
            o_ref[qi * tq:(qi + 1) * tq, :] = out_t.T.astype(BF16)
        scores = nxt


def _fox_attention(q_aug, k_aug, v_t):
    batch, nt, _, tq = v_t.shape
    assert tq == FOX_TILE
    seq = nt * tq
    nh = FOX_HEADS
    q3 = q_aug.reshape(batch, seq, AUG_WIDTH)
    return pl.pallas_call(
        functools.partial(_fox_kernel, tq=tq, nh=nh),
        grid=(batch, B_HEADS // nh),
        in_specs=[pl.BlockSpec((None, seq, nh * LANES), lambda b, hp: (b, 0, hp)),
                  pl.BlockSpec((None, seq, nh * LANES), lambda b, hp: (b, 0, hp)),
                  pl.BlockSpec((None, nt, nh * VT_ROWS, tq), lambda b, hp: (b, 0, hp, 0))],
        out_specs=pl.BlockSpec((None, seq, nh * HEAD_DIM), lambda b, hp: (b, 0, hp)),
        out_shape=jax.ShapeDtypeStruct((batch, seq, B_WIDTH), BF16),
        compiler_params=_params(2),
        name="fox_attn",
    )(q3, k_aug, v_t)


def _bias_placement():
    assert 2 * BIAS_PIECES <= BIAS_STRIDE
    place_q = np.zeros((BIAS_PIECES * LANES, LANES), np.float32)
    place_k = np.zeros((BIAS_PIECES * LANES, LANES), np.float32)
    ones_q = np.zeros((1, LANES), np.float32)
    ones_k = np.zeros((1, LANES), np.float32)
    for h in range(B_HEADS):
        base = h * BIAS_STRIDE
        for piece in range(BIAS_PIECES):
            place_q[piece * LANES + h, base + piece] = 1.0
            ones_k[0, base + piece] = 1.0
            place_k[piece * LANES + h, base + BIAS_PIECES + piece] = -1.0
            ones_q[0, base + BIAS_PIECES + piece] = 1.0
    return (jnp.asarray(place_q, BF16), jnp.asarray(ones_q), jnp.asarray(place_k, BF16), jnp.asarray(ones_k))


def _rope_lane_tables(seq):
    pos = jnp.arange(seq, dtype=F32)
    inv = ROPE_THETA ** (-jnp.arange(0, ROPE_DIM, 2, dtype=F32) / ROPE_DIM)
    ang = pos[:, None] * inv[None, :]
    cos, sin = jnp.cos(ang), jnp.sin(ang)
    ones = jnp.ones((seq, HEAD_DIM - ROPE_DIM), F32)
    zeros = jnp.zeros((seq, HEAD_DIM - ROPE_DIM), F32)
    zh = jnp.zeros((seq, ROPE_HALF), F32)
    cos_h = jnp.concatenate([cos, cos, ones], axis=1)
    sa_h = jnp.concatenate([zh, sin, zeros], axis=1)
    sb_h = jnp.concatenate([-sin, zh, zeros], axis=1)
    rep = LANES // HEAD_DIM
    return tuple(jnp.tile(t, (1, rep)) for t in (cos_h, sa_h, sb_h))


def kernel(x, norm_gains, w_qkv_a, w_o_a, w_q_b, w_o_b, kv_norm, w_kvf, b_f, w_up, conv_w, conv_b, w_down):
    batch, seq, _ = x.shape
    m = batch * seq
    cos_t, sa_t, sb_t = _rope_lane_tables(seq)
    place_q, ones_q, place_k, ones_k = _bias_placement()
    gains = norm_gains.reshape(DEPTH * 4, 1, D_MODEL)
    w_qkv_a, w_o_a, w_q_b, w_o_b, w_up, w_down = (
        t.astype(BF16) for t in (w_qkv_a, w_o_a, w_q_b, w_o_b, w_up, w_down))
    w_kv = w_kvf[:, :2 * B_WIDTH].astype(BF16)
    w_f = jnp.pad(w_kvf[:, 2 * B_WIDTH:], ((0, 0), (0, LANES - B_HEADS))).astype(BF16)
    b_f = jnp.pad(b_f, (0, LANES - B_HEADS)).reshape(1, LANES)
    conv_b = conv_b.reshape(DEPTH, 1, 2 * D_FF)

    h = x
    k_aug = v_sh = c_sh = None
    for l in range(DEPTH):
        g = lambda i: gains[4 * l + i]
        if l < N_A:
            qkv = _qkv_proj(h, g(0), w_qkv_a, l, cos_t, sa_t, sb_t)
            mix_inputs, w_o, o_idx = _dilated_attention(qkv), w_o_a, l
        else:
            if l == N_A:
                k_aug, v_sh, c_sh = _kvf_proj(h, kv_norm.reshape(1, D_MODEL), w_kv, w_f, b_f,
                                              place_k, ones_k)
            j = l - N_A
            q_aug = _q_proj(h.reshape(m, D_MODEL), g(0), w_q_b, j, c_sh.reshape(m, LANES),
                            place_q, ones_q)
            mix_inputs, w_o, o_idx = [_fox_attention(q_aug, k_aug, v_sh)], w_o_b, j
        h = _layer_tail(mix_inputs, w_o, o_idx, h, l, gains, w_up, conv_w, conv_b, w_down)
    return h
```

```python
import functools

import numpy as np

import jax
import jax.numpy as jnp
from jax import lax
from jax.experimental import pallas as pl
from jax.experimental.pallas import tpu as pltpu

D_MODEL = 1024
DEPTH = 4
HEAD_DIM = 64
A_GROUPS = ((128, 1), (512, 4), (2048, 16))
A_HEADS_PER_GROUP = 4
A_HEADS = A_HEADS_PER_GROUP * len(A_GROUPS)
A_WIDTH = A_HEADS * HEAD_DIM
GROUP_WIDTH = A_HEADS_PER_GROUP * HEAD_DIM
B_HEADS = D_MODEL // HEAD_DIM
B_WIDTH = B_HEADS * HEAD_DIM
N_A = DEPTH // 2
D_FF = 2816
CONV_W = 3
ROPE_DIM = HEAD_DIM // 4
ROPE_HALF = ROPE_DIM // 2
ROPE_THETA = 500000.0
BLK = 128
EPS = 1e-6
NEG = -1e30
SCALE = HEAD_DIM ** -0.5
LOG2E = float(np.log2(np.e))

LANES = 128
SUBLANES = 8
MXU_DIM = 256
VMEM_LIMIT = 56 * 1024 * 1024

ROW_TILE = 512
FFN_CHUNK = MXU_DIM
TAIL_SUBTILES = 2
FOX_TILE = 512
VT_ROWS = HEAD_DIM + 16
FOX_HEADS = 4
AUG_WIDTH = B_HEADS * LANES
BIAS_PIECES = 3
BIAS_STRIDE = LANES // B_HEADS
DILATED_UNROLL = (3, 4, 4)

F32 = jnp.float32
BF16 = jnp.bfloat16
NT_DIMS = (((1,), (1,)), ((), ()))


def _params(n_axes):
    return pltpu.CompilerParams(dimension_semantics=("arbitrary",) * n_axes,
                                vmem_limit_bytes=VMEM_LIMIT)


def _resident(shape, index_map):
    return pl.BlockSpec(shape, index_map, pipeline_mode=pl.Buffered(1))


def _rms(x, g):
    return x * lax.rsqrt(jnp.mean(x * x, axis=-1, keepdims=True) + EPS) * g


def _qkv_kernel(x_ref, g_ref, w_ref, cos_ref, sa_ref, sb_ref, o0_ref, o1_ref, o2_ref, ybuf_ref, *, tm):
    xn = _rms(x_ref[...], g_ref[...]).astype(BF16)
    y = jnp.dot(xn, w_ref[...], preferred_element_type=F32)
    cos, sa, sb = cos_ref[...], sa_ref[...], sb_ref[...]
    slabs_per_group = 3 * GROUP_WIDTH // LANES
    slabs_per_kind = GROUP_WIDTH // LANES
    for c in range(3 * A_WIDTH // LANES):
        grp, kind = c // slabs_per_group, (c % slabs_per_group) // slabs_per_kind
        src = kind * A_WIDTH + grp * GROUP_WIDTH + (c % slabs_per_kind) * LANES
        t = y[:, src:src + LANES]
        if kind < 2:
            t = t * cos + pltpu.roll(t, ROPE_HALF, 1) * sa + pltpu.roll(t, LANES - ROPE_HALF, 1) * sb
        if kind == 0:
            t = t * SCALE
        if grp == 0:
            o0_ref[0, :, c * LANES:(c + 1) * LANES] = t.astype(BF16)
        else:
            ybuf_ref[c - slabs_per_group] = t
    for grp, o_ref in ((1, o1_ref), (2, o2_ref)):
        r = A_GROUPS[grp][1]
        for j in range(r):
            for c in range(slabs_per_group):
                o_ref[j, :, c * LANES:(c + 1) * LANES] = ybuf_ref[
                    (grp - 1) * slabs_per_group + c, pl.ds(j, tm // r, stride=r), :].astype(BF16)


def _qkv_proj(h3, g, w_all, idx, cos_t, sa_t, sb_t):
    batch, seq, _ = h3.shape
    tm = ROW_TILE
    n = 3 * A_WIDTH
    gw = 3 * GROUP_WIDTH
    tab = pl.BlockSpec((tm, LANES), lambda b, t: (t, 0))
    (_, r1), (_, r2) = A_GROUPS[1], A_GROUPS[2]
    return pl.pallas_call(
        functools.partial(_qkv_kernel, tm=tm),
        grid=(batch, seq // tm),
        in_specs=[pl.BlockSpec((None, tm, D_MODEL), lambda b, t: (b, t, 0)),
                  _resident((1, D_MODEL), lambda b, t: (0, 0)),
                  _resident((None, D_MODEL, n), lambda b, t: (idx, 0, 0)),
                  tab, tab, tab],
        out_specs=[pl.BlockSpec((None, 1, tm, gw), lambda b, t: (b, 0, t, 0)),
                   pl.BlockSpec((None, r1, tm // r1, gw), lambda b, t: (b, 0, t, 0)),
                   pl.BlockSpec((None, r2, tm // r2, gw), lambda b, t: (b, 0, t, 0))],
        out_shape=[jax.ShapeDtypeStruct((batch, 1, seq, gw), BF16),
                   jax.ShapeDtypeStruct((batch, r1, seq // r1, gw), BF16),
                   jax.ShapeDtypeStruct((batch, r2, seq // r2, gw), BF16)],
        scratch_shapes=[pltpu.VMEM((2 * gw // LANES, tm, LANES), F32)],
        compiler_params=_params(2),
        name="qkv_proj",
    )(h3, g, w_all, cos_t, sa_t, sb_t)


def _dilated_kernel(*refs, plans):
    n = len(plans)
    calls = []
    for g, plan in enumerate(plans):
        q_ref, k_ref, v_ref = refs[3 * g:3 * g + 3]
        o_ref, lse_ref = refs[3 * n + 2 * g:3 * n + 2 * g + 2]
        calls.append(_dilated_calls(q_ref, k_ref, v_ref, o_ref, lse_ref, *plan))
    for i in range(max(len(c) for c in calls)):
        for group_calls in calls:
            if i < len(group_calls):
                group_calls[i]()


def _dilated_calls(q_ref, k_ref, v_ref, o_ref, lse_ref, n_res, nb, n_back, unroll):
    nh = A_HEADS_PER_GROUP
    lane = lax.broadcasted_iota(jnp.int32, (BLK, GROUP_WIDTH), 1)
    head_masks = [(lane >= h * HEAD_DIM) & (lane < (h + 1) * HEAD_DIM) for h in range(nh)]

    def band(width):
        qi = lax.broadcasted_iota(jnp.int32, (nh * BLK, width), 0) & (BLK - 1)
        kj = lax.broadcasted_iota(jnp.int32, (nh * BLK, width), 1)
        rel = qi + (width - BLK) - kj
        return (rel >= 0) & (rel <= n_back)

    def attend(items, width):
        mask = band(width)
        scores = []
        for res, q0, k0 in items:
            q = q_ref[res, pl.ds(q0, BLK), :]
            qs = jnp.concatenate([jnp.where(hm, q, jnp.zeros_like(q)) for hm in head_masks], axis=0)
            scores.append(lax.dot_general(qs, k_ref[res, pl.ds(k0, width), :], NT_DIMS,
                                          preferred_element_type=F32))
        probs = []
        for s in scores:
            s = jnp.where(mask, s, NEG)
            m = jnp.max(s, axis=-1, keepdims=True)
            p = jnp.exp(s - m)
            l = jnp.sum(p, axis=-1, keepdims=True)
            probs.append(((p / l).astype(BF16), m + jnp.log(l)))
        for (res, q0, k0), (pn, lse) in zip(items, probs):
            pv = jnp.dot(pn, v_ref[res, pl.ds(k0, width), :], preferred_element_type=F32)
            o = pv[0:BLK]
            ls = jnp.broadcast_to(lse[0:BLK], (BLK, GROUP_WIDTH))
            for h in range(1, nh):
                o = jnp.where(head_masks[h], pv[h * BLK:(h + 1) * BLK], o)
                ls = jnp.where(head_masks[h], lse[h * BLK:(h + 1) * BLK], ls)
            o_ref[res, pl.ds(q0, BLK), :] = o.astype(BF16)
            lse_ref[res, pl.ds(q0, BLK), :] = ls

    firsts = [(res, 0, 0) for res in range(n_res)]
    rest = [(res, i * BLK, (i - 1) * BLK) for i in range(1, nb) for res in range(n_res)]
    return ([functools.partial(attend, firsts[i:i + unroll], BLK) for i in range(0, n_res, unroll)]
            + [functools.partial(attend, rest[i:i + unroll], 2 * BLK) for i in range(0, len(rest), unroll)])


def _dilated_attention(qkvs):
    batch = qkvs[0].shape[0]
    plans, in_specs, out_specs, out_shapes = [], [], [], []
    for (window, r), unroll, qkv_g in zip(A_GROUPS, DILATED_UNROLL, qkvs):
        ln = qkv_g.shape[2]
        assert ln % BLK == 0
        plans.append((r, ln // BLK, window // r, unroll))
        shape = (None, r, ln, GROUP_WIDTH)
        in_specs += [pl.BlockSpec(shape, functools.partial(lambda b, which: (b, 0, 0, which), which=which))
                     for which in range(3)]
        out_specs.append(pl.BlockSpec(shape, lambda b: (b, 0, 0, 0)))
        out_shapes.append((batch, r, ln, GROUP_WIDTH))
    outs = pl.pallas_call(
        functools.partial(_dilated_kernel, plans=tuple(plans)),
        grid=(batch,),
        in_specs=in_specs,
        out_specs=[s for s in out_specs for _ in range(2)],
        out_shape=[jax.ShapeDtypeStruct(s, dt) for s in out_shapes for dt in (BF16, F32)],
        compiler_params=_params(1),
        name="dilated_attn",
    )(*[qkv_g for qkv_g in qkvs for _ in range(3)])
    return list(outs[0::2]) + list(outs[1::2])


def _mix_dilated(o0_ref, o1_ref, o2_ref, l0_ref, l1_ref, l2_ref, w_ref, obuf_ref, lbuf_ref, *, tm, sub):
    slabs = GROUP_WIDTH // LANES
    for grp, (o_ref, l_ref) in enumerate(((o0_ref, l0_ref), (o1_ref, l1_ref), (o2_ref, l2_ref))):
        r = A_GROUPS[grp][1]
        src = slice(sub * tm // r, (sub + 1) * tm // r)
        for j in range(r):
            rows = pl.ds(j, tm // r, stride=r) if r > 1 else slice(None)
            for c in range(slabs):
                cols = slice(c * LANES, (c + 1) * LANES)
                obuf_ref[grp * slabs + c, rows, :] = o_ref[j, src, cols].astype(F32)
                lbuf_ref[grp * slabs + c, rows, :] = l_ref[j, src, cols]

    def group(buf_ref, grp):
        return jnp.concatenate([buf_ref[grp * slabs + c] for c in range(slabs)], axis=1)

    l0, l1, l2 = (group(lbuf_ref, grp) for grp in range(3))
    mx = jnp.maximum(jnp.maximum(l0, l1), l2)
    e0, e1, e2 = jnp.exp(l0 - mx), jnp.exp(l1 - mx), jnp.exp(l2 - mx)
    den = e0 + e1 + e2
    mixed = jnp.concatenate([(group(obuf_ref, grp) * (e / den)).astype(BF16)
                             for grp, e in enumerate((e0, e1, e2))], axis=1)
    return jnp.dot(mixed, w_ref[...], preferred_element_type=F32)


def _to_strided_rows(x, buf_ref):
    tm, d = x.shape
    n = tm // SUBLANES
    pitch = buf_ref.shape[1] // SUBLANES
    slabs = d // LANES
    for c in range(slabs):
        for s in range(SUBLANES):
            buf_ref[c, s * pitch:s * pitch + n, :] = x[s * n:(s + 1) * n, c * LANES:(c + 1) * LANES]
    return jnp.concatenate(
        [jnp.concatenate([buf_ref[c, pl.ds(i, SUBLANES, stride=pitch), :] for c in range(slabs)], axis=1)
         for i in range(n)], axis=0)


def _store_token_rows(y, buf_ref, out_ref, row0):
    tm, d = y.shape
    n = tm // SUBLANES
    per = n // SUBLANES
    slabs = d // LANES
    for c in range(slabs):
        buf_ref[c] = y[:, c * LANES:(c + 1) * LANES]
    for j in range(n):
        start = SUBLANES * SUBLANES * (j % per) + j // per
        out_ref[row0 + j * SUBLANES:row0 + (j + 1) * SUBLANES, :] = jnp.concatenate(
            [buf_ref[c, pl.ds(start, SUBLANES, stride=SUBLANES), :] for c in range(slabs)], axis=1)


def _tail_kernel(*refs, tm, n_mix, n_sub):
    mix_refs, refs = refs[:n_mix + 1], refs[n_mix + 1:]
    h_ref, g1_ref, g2_ref, wup_ref, cw_ref, cb_ref, wdn_ref, g3_ref, out_ref, carry_ref = refs[:10]
    scratch = refs[10:]
    per_sub = 3
    extra = scratch[n_sub * per_sub:]

    @pl.when(pl.program_id(1) == 0)
    def _():
        carry_ref[...] = jnp.zeros_like(carry_ref)

    ends = {}
    subs = [_tail_sub(mix_refs, extra, h_ref, g1_ref, g2_ref, wup_ref, cw_ref, cb_ref, wdn_ref, g3_ref,
                      out_ref, carry_ref, scratch[i * per_sub:(i + 1) * per_sub], ends,
                      tm=tm, sub=i, n_sub=n_sub) for i in range(n_sub)]
    live = []
    while subs or live:
        if subs:
            live.append(subs.pop(0))
        for gen in list(live):
            if next(gen, "done") == "done":
                live.remove(gen)


def _tail_sub(mix_refs, extra, h_ref, g1_ref, g2_ref, wup_ref, cw_ref, cb_ref, wdn_ref, g3_ref, out_ref,
              carry_ref, own, ends, *, tm, sub, n_sub):
    hmid_ref, pin_ref, pout_ref = own
    rows = slice(sub * tm, (sub + 1) * tm)
    if len(mix_refs) == 2:
        o_ref, w_ref = mix_refs
        mix = jnp.dot(o_ref[rows, :], w_ref[...], preferred_element_type=F32)
    else:
        mix = _mix_dilated(*mix_refs, *extra, tm=tm, sub=sub)
    h = _to_strided_rows(h_ref[rows, :] + _rms(mix, g1_ref[...]), pin_ref)
    xn = _rms(h, g2_ref[...]).astype(BF16)
    last_sublane = lax.broadcasted_iota(jnp.int32, (SUBLANES, FFN_CHUNK), 0) == SUBLANES - 1
    yield

    def conv(col, k):
        cs = slice(col, col + FFN_CHUNK)
        a = jnp.dot(xn, wup_ref[:, cs], preferred_element_type=F32)
        prev = carry_ref[k] if sub == 0 else ends.pop((sub - 1, k))
        if sub == n_sub - 1:
            carry_ref[k] = a[tm - 2 * SUBLANES:tm]
        else:
            ends[(sub, k)] = a[tm - 2 * SUBLANES:tm]
        wrapped = [pltpu.roll(jnp.where(last_sublane, prev[g * SUBLANES:(g + 1) * SUBLANES],
                                        a[tm - (2 - g) * SUBLANES:tm - (1 - g) * SUBLANES]), 1, 0)
                   for g in range(2)]
        a1 = jnp.concatenate([wrapped[1], a[:tm - SUBLANES]], axis=0)
        a2 = jnp.concatenate([wrapped[0], wrapped[1], a[:tm - 2 * SUBLANES]], axis=0)
        return (a2 * cw_ref[0:1, cs] + a1 * cw_ref[1:2, cs] + a * cw_ref[2:3, cs]) + cb_ref[:, cs]

    for c in range(D_FF // FFN_CHUNK):
        gate = conv(c * FFN_CHUNK, 2 * c)
        val = conv(D_FF + c * FFN_CHUNK, 2 * c + 1)
        hmid_ref[:, c * FFN_CHUNK:(c + 1) * FFN_CHUNK] = (
            jax.nn.gelu(gate, approximate=True) * val).astype(BF16)
        yield
    f = jnp.dot(hmid_ref[...], wdn_ref[...], preferred_element_type=F32)
    _store_token_rows(h + _rms(f, g3_ref[...]), pout_ref, out_ref, sub * tm)


def _layer_tail(mix_inputs, w_o_all, o_idx, h3, layer, g_all, w_up, cw, cb, w_down):
    batch, seq, _ = h3.shape
    n_sub = TAIL_SUBTILES
    tm = ROW_TILE // n_sub
    blk = ROW_TILE
    row = pl.BlockSpec((None, blk, D_MODEL), lambda b, t: (b, t, 0))
    gain = lambda i: _resident((None, 1, D_MODEL), lambda b, t: (4 * layer + i, 0, 0))
    if len(mix_inputs) == 1:
        mix_specs = [pl.BlockSpec((None, blk, B_WIDTH), lambda b, t: (b, t, 0))]
        extra_scratch = []
    else:
        mix_specs = 2 * [pl.BlockSpec((None, r, blk // r, GROUP_WIDTH), lambda b, t: (b, 0, t, 0))
                         for _, r in A_GROUPS]
        extra_scratch = 2 * [pltpu.VMEM((A_WIDTH // LANES, tm, LANES), F32)]
    return pl.pallas_call(
        functools.partial(_tail_kernel, tm=tm, n_mix=len(mix_inputs), n_sub=n_sub),
        grid=(batch, seq // blk),
        in_specs=mix_specs + [_resident((None,) + w_o_all.shape[1:], lambda b, t: (o_idx, 0, 0)),
                              row, gain(1), gain(2),
                              _resident((None, D_MODEL, 2 * D_FF), lambda b, t: (layer, 0, 0)),
                              _resident((None, CONV_W, 2 * D_FF), lambda b, t: (layer, 0, 0)),
                              _resident((None, 1, 2 * D_FF), lambda b, t: (layer, 0, 0)),
                              _resident((None, D_FF, D_MODEL), lambda b, t: (layer, 0, 0)),
                              gain(3)],
        out_specs=row,
        out_shape=jax.ShapeDtypeStruct(h3.shape, F32),
        scratch_shapes=([pltpu.VMEM((2 * D_FF // FFN_CHUNK, 2 * SUBLANES, FFN_CHUNK), F32)]
                        + n_sub * ([pltpu.VMEM((tm, D_FF), BF16)]
                                   + [pltpu.VMEM((D_MODEL // LANES, tm + SUBLANES * SUBLANES, LANES), F32),
                                      pltpu.VMEM((D_MODEL // LANES, tm, LANES), F32)])
                        + extra_scratch),
        compiler_params=_params(2),
        name="layer_tail",
    )(*mix_inputs, w_o_all, h3, g_all, g_all, w_up, cw, cb, w_down, g_all)


def _split_bf16(c):
    pieces, rest = [], c
    for _ in range(BIAS_PIECES):
        piece = rest.astype(BF16)
        pieces.append(piece)
        rest = rest - piece.astype(F32)
    return jnp.concatenate(pieces, axis=1)


def _store_augmented(project, scale, c_tile, place_ref, ones_ref, out_ref):
    tm = c_tile.shape[0]
    slab = jnp.dot(_split_bf16(c_tile * LOG2E), place_ref[...], preferred_element_type=F32) + ones_ref[...]
    lane = lax.broadcasted_iota(jnp.int32, (tm, LANES), 1)
    own = lane < HEAD_DIM
    bias_lanes = lane < HEAD_DIM + 2 * BIAS_PIECES
    heads_per_piece = MXU_DIM // HEAD_DIM
    for h in range(B_HEADS):
        if h % heads_per_piece == 0:
            x = project(slice(h * HEAD_DIM, h * HEAD_DIM + MXU_DIM)) * scale
        hp = h % heads_per_piece
        pair = x[:, (hp // 2) * LANES:(hp // 2 + 1) * LANES]
        if h % 2 == 1:
            pair = pltpu.roll(pair, HEAD_DIM, 1)
        shift = (HEAD_DIM - BIAS_STRIDE * h) % LANES
        bias = pltpu.roll(slab, shift, 1) if shift else slab
        out_ref[:, h * LANES:(h + 1) * LANES] = jnp.where(
            own, pair, jnp.where(bias_lanes, bias, 0.0)).astype(BF16)


def _kvf_kernel(x_ref, g_ref, wkv_ref, wf_ref, bf_ref, place_ref, ones_ref, k_ref, v_ref, c_ref,
                carry_ref, *, tm):
    @pl.when(pl.program_id(1) == 0)
    def _():
        carry_ref[...] = jnp.zeros_like(carry_ref)

    xn = _rms(x_ref[...], g_ref[...]).astype(BF16)
    z = jnp.dot(xn, wf_ref[...], preferred_element_type=F32) + bf_ref[...]
    log_f = -(jnp.maximum(-z, 0.0) + jnp.log1p(jnp.exp(-jnp.abs(z))))
    rows = lax.broadcasted_iota(jnp.int32, (tm, LANES), 0)
    acc = log_f
    shift = 1
    while shift < tm:
        acc = acc + jnp.where(rows >= shift, pltpu.roll(acc, shift, 0), 0.0)
        shift *= 2
    acc = acc + carry_ref[...]
    c_ref[...] = acc
    carry_ref[...] = acc[tm - 1:tm, :]
    _store_augmented(lambda cols: jnp.dot(xn, wkv_ref[:, cols], preferred_element_type=F32), 1.0,
                     acc, place_ref, ones_ref, k_ref)
    pad_rows = lax.broadcasted_iota(jnp.int32, (VT_ROWS - HEAD_DIM, tm), 0)
    ones_pad = jnp.where(pad_rows == 0, 1.0, 0.0)
    heads_per_piece = MXU_DIM // HEAD_DIM
    for h0 in range(0, B_HEADS, heads_per_piece):
        cols = slice(B_WIDTH + h0 * HEAD_DIM, B_WIDTH + h0 * HEAD_DIM + MXU_DIM)
        vt = jnp.dot(xn, wkv_ref[:, cols], preferred_element_type=F32).T
        for hp in range(heads_per_piece):
            h = h0 + hp
            v_ref[h * VT_ROWS:(h + 1) * VT_ROWS, :] = jnp.concatenate(
                [vt[hp * HEAD_DIM:(hp + 1) * HEAD_DIM], ones_pad], axis=0).astype(BF16)


def _kvf_proj(h3, g, wkv, wf, bf, place_k, ones_k):
    batch, seq, _ = h3.shape
    tm = ROW_TILE
    row = lambda width: pl.BlockSpec((None, tm, width), lambda b, t: (b, t, 0))
    return pl.pallas_call(
        functools.partial(_kvf_kernel, tm=tm),
        grid=(batch, seq // tm),
        in_specs=[row(D_MODEL),
                  _resident((1, D_MODEL), lambda b, t: (0, 0)),
                  _resident((D_MODEL, 2 * B_WIDTH), lambda b, t: (0, 0)),
                  _resident((D_MODEL, LANES), lambda b, t: (0, 0)),
                  _resident((1, LANES), lambda b, t: (0, 0)),
                  _resident((BIAS_PIECES * LANES, LANES), lambda b, t: (0, 0)),
                  _resident((1, LANES), lambda b, t: (0, 0))],
        out_specs=[row(AUG_WIDTH),
                   pl.BlockSpec((None, None, B_HEADS * VT_ROWS, tm), lambda b, t: (b, t, 0, 0)),
                   row(LANES)],
        out_shape=[jax.ShapeDtypeStruct((batch, seq, AUG_WIDTH), BF16),
                   jax.ShapeDtypeStruct((batch, seq // tm, B_HEADS * VT_ROWS, tm), BF16),
                   jax.ShapeDtypeStruct((batch, seq, LANES), F32)],
        scratch_shapes=[pltpu.VMEM((1, LANES), F32)],
        compiler_params=_params(2),
        name="kvf_proj",
    )(h3, g, wkv, wf, bf, place_k, ones_k)


def _qproj_kernel(x_ref, g_ref, w_ref, c_ref, place_ref, ones_ref, o_ref):
    xn = _rms(x_ref[...], g_ref[...]).astype(BF16)
    _store_augmented(lambda cols: jnp.dot(xn, w_ref[:, cols], preferred_element_type=F32), SCALE * LOG2E,
                     c_ref[...], place_ref, ones_ref, o_ref)


def _q_proj(h2, g, w_all, idx, c2, place_q, ones_q):
    m = h2.shape[0]
    tm = ROW_TILE
    return pl.pallas_call(
        _qproj_kernel,
        grid=(m // tm,),
        in_specs=[pl.BlockSpec((tm, D_MODEL), lambda i: (i, 0)),
                  _resident((1, D_MODEL), lambda i: (0, 0)),
                  _resident((None, D_MODEL, B_WIDTH), lambda i: (idx, 0, 0)),
                  pl.BlockSpec((tm, LANES), lambda i: (i, 0)),
                  _resident((BIAS_PIECES * LANES, LANES), lambda i: (0, 0)),
                  _resident((1, LANES), lambda i: (0, 0))],
        out_specs=pl.BlockSpec((tm, AUG_WIDTH), lambda i: (i, 0)),
        out_shape=jax.ShapeDtypeStruct((m, AUG_WIDTH), BF16),
        compiler_params=_params(1),
        name="q_proj",
    )(h2, g, w_all, c2, place_q, ones_q)


def _fox_kernel(q_ref, k_ref, vt_ref, o_ref, *, tq, nh):
    half = tq // 2

    def scores_of(qi, j, key0, nkeys, query0, diagonal):
        return [lax.dot_general(k_ref[j * tq + key0:j * tq + key0 + nkeys, hh * LANES:(hh + 1) * LANES],
                                q_ref[qi * tq + query0:(qi + 1) * tq, hh * LANES:(hh + 1) * LANES],
                                NT_DIMS, preferred_element_type=F32) for hh in range(nh)]

    def update(step, scores, state):
        _, j, key0, nkeys, query0, diagonal = step
        stats = []
        for hh in range(nh):
            m = state[hh][0][:, query0:]
            s = scores[hh]
            if diagonal:
                visible = (lax.broadcasted_iota(jnp.int32, s.shape, 0)
                           <= lax.broadcasted_iota(jnp.int32, s.shape, 1))
                s = jnp.where(visible, s, NEG)
            m_new = jnp.maximum(m, jnp.max(s, axis=0, keepdims=True))
            stats.append((m_new, jnp.exp2(m - m_new), jnp.exp2(s - m_new).astype(BF16)))
        out = []
        for hh in range(nh):
            m_new, alpha, p = stats[hh]
            vt = vt_ref[j, hh * VT_ROWS:(hh + 1) * VT_ROWS, key0:key0 + nkeys]
            m_old, acc_old = state[hh]
            acc = alpha * acc_old[:, query0:] + jnp.dot(vt, p, preferred_element_type=F32)
            if query0:
                m_new = jnp.concatenate([m_old[:, :query0], m_new], axis=1)
                acc = jnp.concatenate([acc_old[:, :query0], acc], axis=1)
            out.append((m_new, acc))
        return tuple(out)

    steps = []
    for qi in range(q_ref.shape[0] // tq):
        steps += [(qi, j, 0, tq, 0, False) for j in range(qi)]
        steps += [(qi, qi, 0, half, 0, True), (qi, qi, half, half, half, True)]
    scores = scores_of(*steps[0])
    state = None
    for n, step in enumerate(steps):
        nxt = scores_of(*steps[n + 1]) if n + 1 < len(steps) else None
        qi, j, key0 = step[:3]
        if j == 0 and key0 == 0:
            state = tuple((jnp.full((1, tq), NEG, F32), jnp.zeros((VT_ROWS, tq), F32)) for _ in range(nh))
        state = update(step, scores, state)
        if j == qi and key0:
            out_t = jnp.concatenate([acc[:HEAD_DIM] / acc[HEAD_DIM:HEAD_DIM + 1] for _, acc in state], axis=0)
            o_ref[qi * tq:(qi + 1) * tq, :] = out_t.T.astype(BF16)
        scores = nxt


def _fox_attention(q_aug, k_aug, v_t):
    batch, nt, _, tq = v_t.shape
    assert tq == FOX_TILE
    seq = nt * tq
    nh = FOX_HEADS
    q3 = q_aug.reshape(batch, seq, AUG_WIDTH)
    return pl.pallas_call(
        functools.partial(_fox_kernel, tq=tq, nh=nh),
        grid=(batch, B_HEADS // nh),
        in_specs=[pl.BlockSpec((None, seq, nh * LANES), lambda b, hp: (b, 0, hp)),
                  pl.BlockSpec((None, seq, nh * LANES), lambda b, hp: (b, 0, hp)),
                  pl.BlockSpec((None, nt, nh * VT_ROWS, tq), lambda b, hp: (b, 0, hp, 0))],
        out_specs=pl.BlockSpec((None, seq, nh * HEAD_DIM), lambda b, hp: (b, 0, hp)),
        out_shape=jax.ShapeDtypeStruct((batch, seq, B_WIDTH), BF16),
        compiler_params=_params(2),
        name="fox_attn",
    )(q3, k_aug, v_t)


def _bias_placement():
    assert 2 * BIAS_PIECES <= BIAS_STRIDE
    place_q = np.zeros((BIAS_PIECES * LANES, LANES), np.float32)
    place_k = np.zeros((BIAS_PIECES * LANES, LANES), np.float32)
    ones_q = np.zeros((1, LANES), np.float32)
    ones_k = np.zeros((1, LANES), np.float32)
    for h in range(B_HEADS):
        base = h * BIAS_STRIDE
        for piece in range(BIAS_PIECES):
            place_q[piece * LANES + h, base + piece] = 1.0
            ones_k[0, base + piece] = 1.0
            place_k[piece * LANES + h, base + BIAS_PIECES + piece] = -1.0
            ones_q[0, base + BIAS_PIECES + piece] = 1.0
    return (jnp.asarray(place_q, BF16), jnp.asarray(ones_q), jnp.asarray(place_k, BF16), jnp.asarray(ones_k))


def _rope_lane_tables(seq):
    pos = jnp.arange(seq, dtype=F32)
    inv = ROPE_THETA ** (-jnp.arange(0, ROPE_DIM, 2, dtype=F32) / ROPE_DIM)
    ang = pos[:, None] * inv[None, :]
    cos, sin = jnp.cos(ang), jnp.sin(ang)
    ones = jnp.ones((seq, HEAD_DIM - ROPE_DIM), F32)
    zeros = jnp.zeros((seq, HEAD_DIM - ROPE_DIM), F32)
    zh = jnp.zeros((seq, ROPE_HALF), F32)
    cos_h = jnp.concatenate([cos, cos, ones], axis=1)
    sa_h = jnp.concatenate([zh, sin, zeros], axis=1)
    sb_h = jnp.concatenate([-sin, zh, zeros], axis=1)
    rep = LANES // HEAD_DIM
    return tuple(jnp.tile(t, (1, rep)) for t in (cos_h, sa_h, sb_h))


def kernel(x, norm_gains, w_qkv_a, w_o_a, w_q_b, w_o_b, kv_norm, w_kvf, b_f, w_up, conv_w, conv_b, w_down):
    batch, seq, _ = x.shape
    m = batch * seq
    cos_t, sa_t, sb_t = _rope_lane_tables(seq)
    place_q, ones_q, place_k, ones_k = _bias_placement()
    gains = norm_gains.reshape(DEPTH * 4, 1, D_MODEL)
    w_qkv_a, w_o_a, w_q_b, w_o_b, w_up, w_down = (
        t.astype(BF16) for t in (w_qkv_a, w_o_a, w_q_b, w_o_b, w_up, w_down))
    w_kv = w_kvf[:, :2 * B_WIDTH].astype(BF16)
    w_f = jnp.pad(w_kvf[:, 2 * B_WIDTH:], ((0, 0), (0, LANES - B_HEADS))).astype(BF16)
    b_f = jnp.pad(b_f, (0, LANES - B_HEADS)).reshape(1, LANES)
    conv_b = conv_b.reshape(DEPTH, 1, 2 * D_FF)

    h = x
    k_aug = v_sh = c_sh = None
    for l in range(DEPTH):
        g = lambda i: gains[4 * l + i]
        if l < N_A:
            qkv = _qkv_proj(h, g(0), w_qkv_a, l, cos_t, sa_t, sb_t)
            mix_inputs, w_o, o_idx = _dilated_attention(qkv), w_o_a, l
        else:
            if l == N_A:
                k_aug, v_sh, c_sh = _kvf_proj(h, kv_norm.reshape(1, D_MODEL), w_kv, w_f, b_f,
                                              place_k, ones_k)
            j = l - N_A
            q_aug = _q_proj(h.reshape(m, D_MODEL), g(0), w_q_b, j, c_sh.reshape(m, LANES),
                            place_q, ones_q)
            mix_inputs, w_o, o_idx = [_fox_attention(q_aug, k_aug, v_sh)], w_o_b, j
        h = _layer_tail(mix_inputs, w_o, o_idx, h, l, gains, w_up, conv_w, conv_b, w_down)
    return h
```

```python
import functools

import numpy as np

import jax
import jax.numpy as jnp
from jax import lax
from jax.experimental import pallas as pl
from jax.experimental.pallas import tpu as pltpu

D_MODEL = 1024
DEPTH = 4
HEAD_DIM = 64
A_GROUPS = ((128, 1), (512, 4), (2048, 16))
A_HEADS_PER_GROUP = 4
A_HEADS = A_HEADS_PER_GROUP * len(A_GROUPS)
A_WIDTH = A_HEADS * HEAD_DIM
GROUP_WIDTH = A_HEADS_PER_GROUP * HEAD_DIM
B_HEADS = D_MODEL // HEAD_DIM
B_WIDTH = B_HEADS * HEAD_DIM
N_A = DEPTH // 2
D_FF = 2816
CONV_W = 3
ROPE_DIM = HEAD_DIM // 4
ROPE_HALF = ROPE_DIM // 2
ROPE_THETA = 500000.0
BLK = 128
EPS = 1e-6
NEG = -1e30
SCALE = HEAD_DIM ** -0.5
LOG2E = float(np.log2(np.e))

LANES = 128
SUBLANES = 8
MXU_DIM = 256
VMEM_LIMIT = 56 * 1024 * 1024

ROW_TILE = 512
FFN_CHUNK = MXU_DIM
TAIL_SUBTILES = 2
FOX_TILE = 512
VT_ROWS = HEAD_DIM + 16
FOX_HEADS = 4
BIAS_PIECES = 3
BIAS_STRIDE = LANES // B_HEADS
DILATED_UNROLL = (3, 4, 4)

F32 = jnp.float32
BF16 = jnp.bfloat16
NT_DIMS = (((1,), (1,)), ((), ()))


def _params(n_axes):
    return pltpu.CompilerParams(dimension_semantics=("arbitrary",) * n_axes,
                                vmem_limit_bytes=VMEM_LIMIT)


def _resident(shape, index_map):
    return pl.BlockSpec(shape, index_map, pipeline_mode=pl.Buffered(1))


def _rms(x, g):
    return x * lax.rsqrt(jnp.mean(x * x, axis=-1, keepdims=True) + EPS) * g


def _qkv_kernel(x_ref, g_ref, w_ref, cos_ref, sa_ref, sb_ref, o0_ref, o1_ref, o2_ref, ybuf_ref, *, tm):
    xn = _rms(x_ref[...], g_ref[...]).astype(BF16)
    y = jnp.dot(xn, w_ref[...], preferred_element_type=F32)
    cos, sa, sb = cos_ref[...], sa_ref[...], sb_ref[...]
    slabs_per_group = 3 * GROUP_WIDTH // LANES
    slabs_per_kind = GROUP_WIDTH // LANES
    for c in range(3 * A_WIDTH // LANES):
        grp, kind = c // slabs_per_group, (c % slabs_per_group) // slabs_per_kind
        src = kind * A_WIDTH + grp * GROUP_WIDTH + (c % slabs_per_kind) * LANES
        t = y[:, src:src + LANES]
        if kind < 2:
            t = t * cos + pltpu.roll(t, ROPE_HALF, 1) * sa + pltpu.roll(t, LANES - ROPE_HALF, 1) * sb
        if kind == 0:
            t = t * SCALE
        if grp == 0:
            o0_ref[0, :, c * LANES:(c + 1) * LANES] = t.astype(BF16)
        else:
            ybuf_ref[c - slabs_per_group] = t
    for grp, o_ref in ((1, o1_ref), (2, o2_ref)):
        r = A_GROUPS[grp][1]
        for j in range(r):
            for c in range(slabs_per_group):
                o_ref[j, :, c * LANES:(c + 1) * LANES] = ybuf_ref[
                    (grp - 1) * slabs_per_group + c, pl.ds(j, tm // r, stride=r), :].astype(BF16)


def _qkv_proj(h3, g, w_all, idx, cos_t, sa_t, sb_t):
    batch, seq, _ = h3.shape
    tm = ROW_TILE
    n = 3 * A_WIDTH
    gw = 3 * GROUP_WIDTH
    tab = pl.BlockSpec((tm, LANES), lambda b, t: (t, 0))
    (_, r1), (_, r2) = A_GROUPS[1], A_GROUPS[2]
    return pl.pallas_call(
        functools.partial(_qkv_kernel, tm=tm),
        grid=(batch, seq // tm),
        in_specs=[pl.BlockSpec((None, tm, D_MODEL), lambda b, t: (b, t, 0)),
                  _resident((1, D_MODEL), lambda b, t: (0, 0)),
                  _resident((None, D_MODEL, n), lambda b, t: (idx, 0, 0)),
                  tab, tab, tab],
        out_specs=[pl.BlockSpec((None, 1, tm, gw), lambda b, t: (b, 0, t, 0)),
                   pl.BlockSpec((None, r1, tm // r1, gw), lambda b, t: (b, 0, t, 0)),
                   pl.BlockSpec((None, r2, tm // r2, gw), lambda b, t: (b, 0, t, 0))],
        out_shape=[jax.ShapeDtypeStruct((batch, 1, seq, gw), BF16),
                   jax.ShapeDtypeStruct((batch, r1, seq // r1, gw), BF16),
                   jax.ShapeDtypeStruct((batch, r2, seq // r2, gw), BF16)],
        scratch_shapes=[pltpu.VMEM((2 * gw // LANES, tm, LANES), F32)],
        compiler_params=_params(2),
        name="qkv_proj",
    )(h3, g, w_all, cos_t, sa_t, sb_t)


def _dilated_kernel(*refs, plans):
    n = len(plans)
    calls = []
    for g, plan in enumerate(plans):
        q_ref, k_ref, v_ref = refs[3 * g:3 * g + 3]
        o_ref, lse_ref = refs[3 * n + 2 * g:3 * n + 2 * g + 2]
        calls.append(_dilated_calls(q_ref, k_ref, v_ref, o_ref, lse_ref, *plan))
    for i in range(max(len(c) for c in calls)):
        for group_calls in calls:
            if i < len(group_calls):
                group_calls[i]()


def _dilated_calls(q_ref, k_ref, v_ref, o_ref, lse_ref, n_res, nb, n_back, unroll):
    nh = A_HEADS_PER_GROUP
    lane = lax.broadcasted_iota(jnp.int32, (BLK, GROUP_WIDTH), 1)
    head_masks = [(lane >= h * HEAD_DIM) & (lane < (h + 1) * HEAD_DIM) for h in range(nh)]

    def band(width):
        qi = lax.broadcasted_iota(jnp.int32, (nh * BLK, width), 0) & (BLK - 1)
        kj = lax.broadcasted_iota(jnp.int32, (nh * BLK, width), 1)
        rel = qi + (width - BLK) - kj
        return (rel >= 0) & (rel <= n_back)

    def attend(items, width):
        mask = band(width)
        scores = []
        for res, q0, k0 in items:
            q = q_ref[res, pl.ds(q0, BLK), :]
            qs = jnp.concatenate([jnp.where(hm, q, jnp.zeros_like(q)) for hm in head_masks], axis=0)
            scores.append(lax.dot_general(qs, k_ref[res, pl.ds(k0, width), :], NT_DIMS,
                                          preferred_element_type=F32))
        probs = []
        for s in scores:
            s = jnp.where(mask, s, NEG)
            m = jnp.max(s, axis=-1, keepdims=True)
            p = jnp.exp(s - m)
            l = jnp.sum(p, axis=-1, keepdims=True)
            probs.append(((p / l).astype(BF16), m + jnp.log(l)))
        for (res, q0, k0), (pn, lse) in zip(items, probs):
            pv = jnp.dot(pn, v_ref[res, pl.ds(k0, width), :], preferred_element_type=F32)
            o = pv[0:BLK]
            ls = jnp.broadcast_to(lse[0:BLK], (BLK, GROUP_WIDTH))
            for h in range(1, nh):
                o = jnp.where(head_masks[h], pv[h * BLK:(h + 1) * BLK], o)
                ls = jnp.where(head_masks[h], lse[h * BLK:(h + 1) * BLK], ls)
            o_ref[res, pl.ds(q0, BLK), :] = o.astype(BF16)
            lse_ref[res, pl.ds(q0, BLK), :] = ls

    firsts = [(res, 0, 0) for res in range(n_res)]
    rest = [(res, i * BLK, (i - 1) * BLK) for i in range(1, nb) for res in range(n_res)]
    return ([functools.partial(attend, firsts[i:i + unroll], BLK) for i in range(0, n_res, unroll)]
            + [functools.partial(attend, rest[i:i + unroll], 2 * BLK) for i in range(0, len(rest), unroll)])


def _dilated_attention(qkvs):
    batch = qkvs[0].shape[0]
    plans, in_specs, out_specs, out_shapes = [], [], [], []
    for (window, r), unroll, qkv_g in zip(A_GROUPS, DILATED_UNROLL, qkvs):
        ln = qkv_g.shape[2]
        assert ln % BLK == 0
        plans.append((r, ln // BLK, window // r, unroll))
        shape = (None, r, ln, GROUP_WIDTH)
        in_specs += [pl.BlockSpec(shape, functools.partial(lambda b, which: (b, 0, 0, which), which=which))
                     for which in range(3)]
        out_specs.append(pl.BlockSpec(shape, lambda b: (b, 0, 0, 0)))
        out_shapes.append((batch, r, ln, GROUP_WIDTH))
    outs = pl.pallas_call(
        functools.partial(_dilated_kernel, plans=tuple(plans)),
        grid=(batch,),
        in_specs=in_specs,
        out_specs=[s for s in out_specs for _ in range(2)],
        out_shape=[jax.ShapeDtypeStruct(s, dt) for s in out_shapes for dt in (BF16, F32)],
        compiler_params=_params(1),
        name="dilated_attn",
    )(*[qkv_g for qkv_g in qkvs for _ in range(3)])
    return list(outs[0::2]) + list(outs[1::2])


def _mix_dilated(o0_ref, o1_ref, o2_ref, l0_ref, l1_ref, l2_ref, w_ref, obuf_ref, lbuf_ref, *, tm, sub):
    slabs = GROUP_WIDTH // LANES
    for grp, (o_ref, l_ref) in enumerate(((o0_ref, l0_ref), (o1_ref, l1_ref), (o2_ref, l2_ref))):
        r = A_GROUPS[grp][1]
        src = slice(sub * tm // r, (sub + 1) * tm // r)
        for j in range(r):
            rows = pl.ds(j, tm // r, stride=r) if r > 1 else slice(None)
            for c in range(slabs):
                cols = slice(c * LANES, (c + 1) * LANES)
                obuf_ref[grp * slabs + c, rows, :] = o_ref[j, src, cols].astype(F32)
                lbuf_ref[grp * slabs + c, rows, :] = l_ref[j, src, cols]

    def group(buf_ref, grp):
        return jnp.concatenate([buf_ref[grp * slabs + c] for c in range(slabs)], axis=1)

    l0, l1, l2 = (group(lbuf_ref, grp) for grp in range(3))
    mx = jnp.maximum(jnp.maximum(l0, l1), l2)
    e0, e1, e2 = jnp.exp(l0 - mx), jnp.exp(l1 - mx), jnp.exp(l2 - mx)
    den = e0 + e1 + e2
    mixed = jnp.concatenate([(group(obuf_ref, grp) * (e / den)).astype(BF16)
                             for grp, e in enumerate((e0, e1, e2))], axis=1)
    return jnp.dot(mixed, w_ref[...], preferred_element_type=F32)


def _to_strided_rows(x, buf_ref):
    tm, d = x.shape
    n = tm // SUBLANES
    pitch = buf_ref.shape[1] // SUBLANES
    slabs = d // LANES
    for c in range(slabs):
        for s in range(SUBLANES):
            buf_ref[c, s * pitch:s * pitch + n, :] = x[s * n:(s + 1) * n, c * LANES:(c + 1) * LANES]
    return jnp.concatenate(
        [jnp.concatenate([buf_ref[c, pl.ds(i, SUBLANES, stride=pitch), :] for c in range(slabs)], axis=1)
         for i in range(n)], axis=0)


def _store_token_rows(y, buf_ref, out_ref, row0):
    tm, d = y.shape
    n = tm // SUBLANES
    per = n // SUBLANES
    slabs = d // LANES
    for c in range(slabs):
        buf_ref[c] = y[:, c * LANES:(c + 1) * LANES]
    for j in range(n):
        start = SUBLANES * SUBLANES * (j % per) + j // per
        out_ref[row0 + j * SUBLANES:row0 + (j + 1) * SUBLANES, :] = jnp.concatenate(
            [buf_ref[c, pl.ds(start, SUBLANES, stride=SUBLANES), :] for c in range(slabs)], axis=1)


def _tail_kernel(*refs, tm, n_mix, n_sub):
    mix_refs, refs = refs[:n_mix + 1], refs[n_mix + 1:]
    h_ref, g1_ref, g2_ref, wup_ref, cw_ref, cb_ref, wdn_ref, g3_ref, out_ref, carry_ref = refs[:10]
    scratch = refs[10:]
    per_sub = 3
    extra = scratch[n_sub * per_sub:]

    @pl.when(pl.program_id(1) == 0)
    def _():
        carry_ref[...] = jnp.zeros_like(carry_ref)

    ends = {}
    subs = [_tail_sub(mix_refs, extra, h_ref, g1_ref, g2_ref, wup_ref, cw_ref, cb_ref, wdn_ref, g3_ref,
                      out_ref, carry_ref, scratch[i * per_sub:(i + 1) * per_sub], ends,
                      tm=tm, sub=i, n_sub=n_sub) for i in range(n_sub)]
    live = []
    while subs or live:
        if subs:
            live.append(subs.pop(0))
        for gen in list(live):
            if next(gen, "done") == "done":
                live.remove(gen)


def _tail_sub(mix_refs, extra, h_ref, g1_ref, g2_ref, wup_ref, cw_ref, cb_ref, wdn_ref, g3_ref, out_ref,
              carry_ref, own, ends, *, tm, sub, n_sub):
    hmid_ref, pin_ref, pout_ref = own
    rows = slice(sub * tm, (sub + 1) * tm)
    if len(mix_refs) == 2:
        o_ref, w_ref = mix_refs
        o = jnp.concatenate([o_ref[grp, rows, :] for grp in range(o_ref.shape[0])], axis=1)
        mix = jnp.dot(o, w_ref[...], preferred_element_type=F32)
    else:
        mix = _mix_dilated(*mix_refs, *extra, tm=tm, sub=sub)
    h = _to_strided_rows(h_ref[rows, :] + _rms(mix, g1_ref[...]), pin_ref)
    xn = _rms(h, g2_ref[...]).astype(BF16)
    last_sublane = lax.broadcasted_iota(jnp.int32, (SUBLANES, FFN_CHUNK), 0) == SUBLANES - 1
    yield

    def conv(col, k):
        cs = slice(col, col + FFN_CHUNK)
        a = jnp.dot(xn, wup_ref[:, cs], preferred_element_type=F32)
        prev = carry_ref[k] if sub == 0 else ends.pop((sub - 1, k))
        if sub == n_sub - 1:
            carry_ref[k] = a[tm - 2 * SUBLANES:tm]
        else:
            ends[(sub, k)] = a[tm - 2 * SUBLANES:tm]
        wrapped = [pltpu.roll(jnp.where(last_sublane, prev[g * SUBLANES:(g + 1) * SUBLANES],
                                        a[tm - (2 - g) * SUBLANES:tm - (1 - g) * SUBLANES]), 1, 0)
                   for g in range(2)]
        a1 = jnp.concatenate([wrapped[1], a[:tm - SUBLANES]], axis=0)
        a2 = jnp.concatenate([wrapped[0], wrapped[1], a[:tm - 2 * SUBLANES]], axis=0)
        return (a2 * cw_ref[0:1, cs] + a1 * cw_ref[1:2, cs] + a * cw_ref[2:3, cs]) + cb_ref[:, cs]

    for c in range(D_FF // FFN_CHUNK):
        gate = conv(c * FFN_CHUNK, 2 * c)
        val = conv(D_FF + c * FFN_CHUNK, 2 * c + 1)
        hmid_ref[:, c * FFN_CHUNK:(c + 1) * FFN_CHUNK] = (
            jax.nn.gelu(gate, approximate=True) * val).astype(BF16)
        yield
    f = jnp.dot(hmid_ref[...], wdn_ref[...], preferred_element_type=F32)
    _store_token_rows(h + _rms(f, g3_ref[...]), pout_ref, out_ref, sub * tm)


def _layer_tail(mix_inputs, w_o_all, o_idx, h3, layer, g_all, w_up, cw, cb, w_down):
    batch, seq, _ = h3.shape
    n_sub = TAIL_SUBTILES
    tm = ROW_TILE // n_sub
    blk = ROW_TILE
    row = pl.BlockSpec((None, blk, D_MODEL), lambda b, t: (b, t, 0))
    gain = lambda i: _resident((None, 1, D_MODEL), lambda b, t: (4 * layer + i, 0, 0))
    if len(mix_inputs) == 1:
        groups, width = mix_inputs[0].shape[1], mix_inputs[0].shape[3]
        mix_specs = [pl.BlockSpec((None, groups, blk, width), lambda b, t: (b, 0, t, 0))]
        extra_scratch = []
    else:
        mix_specs = 2 * [pl.BlockSpec((None, r, blk // r, GROUP_WIDTH), lambda b, t: (b, 0, t, 0))
                         for _, r in A_GROUPS]
        extra_scratch = 2 * [pltpu.VMEM((A_WIDTH // LANES, tm, LANES), F32)]
    return pl.pallas_call(
        functools.partial(_tail_kernel, tm=tm, n_mix=len(mix_inputs), n_sub=n_sub),
        grid=(batch, seq // blk),
        in_specs=mix_specs + [_resident((None,) + w_o_all.shape[1:], lambda b, t: (o_idx, 0, 0)),
                              row, gain(1), gain(2),
                              _resident((None, D_MODEL, 2 * D_FF), lambda b, t: (layer, 0, 0)),
                              _resident((None, CONV_W, 2 * D_FF), lambda b, t: (layer, 0, 0)),
                              _resident((None, 1, 2 * D_FF), lambda b, t: (layer, 0, 0)),
                              _resident((None, D_FF, D_MODEL), lambda b, t: (layer, 0, 0)),
                              gain(3)],
        out_specs=row,
        out_shape=jax.ShapeDtypeStruct(h3.shape, F32),
        scratch_shapes=([pltpu.VMEM((2 * D_FF // FFN_CHUNK, 2 * SUBLANES, FFN_CHUNK), F32)]
                        + n_sub * ([pltpu.VMEM((tm, D_FF), BF16)]
                                   + [pltpu.VMEM((D_MODEL // LANES, tm + SUBLANES * SUBLANES, LANES), F32),
                                      pltpu.VMEM((D_MODEL // LANES, tm, LANES), F32)])
                        + extra_scratch),
        compiler_params=_params(2),
        name="layer_tail",
    )(*mix_inputs, w_o_all, h3, g_all, g_all, w_up, cw, cb, w_down, g_all)


def _split_bf16(c):
    pieces, rest = [], c
    for _ in range(BIAS_PIECES):
        piece = rest.astype(BF16)
        pieces.append(piece)
        rest = rest - piece.astype(F32)
    return jnp.concatenate(pieces, axis=1)


def _store_augmented(project, scale, c_tile, place_ref, ones_ref, out_ref):
    tm = c_tile.shape[0]
    slab = jnp.dot(_split_bf16(c_tile * LOG2E), place_ref[...], preferred_element_type=F32) + ones_ref[...]
    lane = lax.broadcasted_iota(jnp.int32, (tm, LANES), 1)
    own = lane < HEAD_DIM
    bias_lanes = lane < HEAD_DIM + 2 * BIAS_PIECES
    heads_per_piece = MXU_DIM // HEAD_DIM
    for h in range(B_HEADS):
        if h % heads_per_piece == 0:
            x = project(slice(h * HEAD_DIM, h * HEAD_DIM + MXU_DIM)) * scale
        hp = h % heads_per_piece
        pair = x[:, (hp // 2) * LANES:(hp // 2 + 1) * LANES]
        if h % 2 == 1:
            pair = pltpu.roll(pair, HEAD_DIM, 1)
        shift = (HEAD_DIM - BIAS_STRIDE * h) % LANES
        bias = pltpu.roll(slab, shift, 1) if shift else slab
        lane0 = (h % FOX_HEADS) * LANES
        out_ref[h // FOX_HEADS, :, lane0:lane0 + LANES] = jnp.where(
            own, pair, jnp.where(bias_lanes, bias, 0.0)).astype(BF16)


def _kvf_kernel(x_ref, g_ref, wkv_ref, wf_ref, bf_ref, place_ref, ones_ref, k_ref, v_ref, c_ref,
                carry_ref, *, tm):
    @pl.when(pl.program_id(1) == 0)
    def _():
        carry_ref[...] = jnp.zeros_like(carry_ref)

    xn = _rms(x_ref[...], g_ref[...]).astype(BF16)
    z = jnp.dot(xn, wf_ref[...], preferred_element_type=F32) + bf_ref[...]
    log_f = -(jnp.maximum(-z, 0.0) + jnp.log1p(jnp.exp(-jnp.abs(z))))
    rows = lax.broadcasted_iota(jnp.int32, (tm, LANES), 0)
    acc = log_f
    shift = 1
    while shift < tm:
        acc = acc + jnp.where(rows >= shift, pltpu.roll(acc, shift, 0), 0.0)
        shift *= 2
    acc = acc + carry_ref[...]
    c_ref[...] = acc
    carry_ref[...] = acc[tm - 1:tm, :]
    _store_augmented(lambda cols: jnp.dot(xn, wkv_ref[:, cols], preferred_element_type=F32), 1.0,
                     acc, place_ref, ones_ref, k_ref)
    pad_rows = lax.broadcasted_iota(jnp.int32, (VT_ROWS - HEAD_DIM, tm), 0)
    ones_pad = jnp.where(pad_rows == 0, 1.0, 0.0)
    heads_per_piece = MXU_DIM // HEAD_DIM
    for h0 in range(0, B_HEADS, heads_per_piece):
        cols = slice(B_WIDTH + h0 * HEAD_DIM, B_WIDTH + h0 * HEAD_DIM + MXU_DIM)
        vt = jnp.dot(xn, wkv_ref[:, cols], preferred_element_type=F32).T
        for hp in range(heads_per_piece):
            h = h0 + hp
            v_ref[h * VT_ROWS:(h + 1) * VT_ROWS, :] = jnp.concatenate(
                [vt[hp * HEAD_DIM:(hp + 1) * HEAD_DIM], ones_pad], axis=0).astype(BF16)


def _kvf_proj(h3, g, wkv, wf, bf, place_k, ones_k):
    batch, seq, _ = h3.shape
    tm = ROW_TILE
    row = lambda width: pl.BlockSpec((None, tm, width), lambda b, t: (b, t, 0))
    return pl.pallas_call(
        functools.partial(_kvf_kernel, tm=tm),
        grid=(batch, seq // tm),
        in_specs=[row(D_MODEL),
                  _resident((1, D_MODEL), lambda b, t: (0, 0)),
                  _resident((D_MODEL, 2 * B_WIDTH), lambda b, t: (0, 0)),
                  _resident((D_MODEL, LANES), lambda b, t: (0, 0)),
                  _resident((1, LANES), lambda b, t: (0, 0)),
                  _resident((BIAS_PIECES * LANES, LANES), lambda b, t: (0, 0)),
                  _resident((1, LANES), lambda b, t: (0, 0))],
        out_specs=[pl.BlockSpec((None, B_HEADS // FOX_HEADS, tm, FOX_HEADS * LANES), lambda b, t: (b, 0, t, 0)),
                   pl.BlockSpec((None, None, B_HEADS * VT_ROWS, tm), lambda b, t: (b, t, 0, 0)),
                   row(LANES)],
        out_shape=[jax.ShapeDtypeStruct((batch, B_HEADS // FOX_HEADS, seq, FOX_HEADS * LANES), BF16),
                   jax.ShapeDtypeStruct((batch, seq // tm, B_HEADS * VT_ROWS, tm), BF16),
                   jax.ShapeDtypeStruct((batch, seq, LANES), F32)],
        scratch_shapes=[pltpu.VMEM((1, LANES), F32)],
        compiler_params=_params(2),
        name="kvf_proj",
    )(h3, g, wkv, wf, bf, place_k, ones_k)


def _qproj_kernel(x_ref, g_ref, w_ref, c_ref, place_ref, ones_ref, o_ref):
    xn = _rms(x_ref[...], g_ref[...]).astype(BF16)
    _store_augmented(lambda cols: jnp.dot(xn, w_ref[:, cols], preferred_element_type=F32), SCALE * LOG2E,
                     c_ref[...], place_ref, ones_ref, o_ref)


def _q_proj(h2, g, w_all, idx, c2, place_q, ones_q, seq):
    m = h2.shape[0]
    tm = ROW_TILE
    nt = seq // tm
    groups = B_HEADS // FOX_HEADS
    return pl.pallas_call(
        _qproj_kernel,
        grid=(m // tm,),
        in_specs=[pl.BlockSpec((tm, D_MODEL), lambda i: (i, 0)),
                  _resident((1, D_MODEL), lambda i: (0, 0)),
                  _resident((None, D_MODEL, B_WIDTH), lambda i: (idx, 0, 0)),
                  pl.BlockSpec((tm, LANES), lambda i: (i, 0)),
                  _resident((BIAS_PIECES * LANES, LANES), lambda i: (0, 0)),
                  _resident((1, LANES), lambda i: (0, 0))],
        out_specs=pl.BlockSpec((None, groups, tm, FOX_HEADS * LANES), lambda i: (i // nt, 0, i % nt, 0)),
        out_shape=jax.ShapeDtypeStruct((m // seq, groups, seq, FOX_HEADS * LANES), BF16),
        compiler_params=_params(1),
        name="q_proj",
    )(h2, g, w_all, c2, place_q, ones_q)


def _fox_kernel(q_ref, k_ref, vt_ref, o_ref, *, tq, nh):
    half = tq // 2

    def scores_of(qi, j, key0, nkeys, query0, diagonal):
        return [lax.dot_general(k_ref[j * tq + key0:j * tq + key0 + nkeys, hh * LANES:(hh + 1) * LANES],
                                q_ref[qi * tq + query0:(qi + 1) * tq, hh * LANES:(hh + 1) * LANES],
                                NT_DIMS, preferred_element_type=F32) for hh in range(nh)]

    def update(step, scores, state):
        _, j, key0, nkeys, query0, diagonal = step
        stats = []
        for hh in range(nh):
            m = state[hh][0][:, query0:]
            s = scores[hh]
            if diagonal:
                visible = (lax.broadcasted_iota(jnp.int32, s.shape, 0)
                           <= lax.broadcasted_iota(jnp.int32, s.shape, 1))
                s = jnp.where(visible, s, NEG)
            m_new = jnp.maximum(m, jnp.max(s, axis=0, keepdims=True))
            stats.append((m_new, jnp.exp2(m - m_new), jnp.exp2(s - m_new).astype(BF16)))
        out = []
        for hh in range(nh):
            m_new, alpha, p = stats[hh]
            vt = vt_ref[j, hh * VT_ROWS:(hh + 1) * VT_ROWS, key0:key0 + nkeys]
            m_old, acc_old = state[hh]
            acc = alpha * acc_old[:, query0:] + jnp.dot(vt, p, preferred_element_type=F32)
            if query0:
                m_new = jnp.concatenate([m_old[:, :query0], m_new], axis=1)
                acc = jnp.concatenate([acc_old[:, :query0], acc], axis=1)
            out.append((m_new, acc))
        return tuple(out)

    steps = []
    for qi in range(q_ref.shape[0] // tq):
        steps += [(qi, j, 0, tq, 0, False) for j in range(qi)]
        steps += [(qi, qi, 0, half, 0, True), (qi, qi, half, half, half, True)]
    scores = scores_of(*steps[0])
    state = None
    for n, step in enumerate(steps):
        nxt = scores_of(*steps[n + 1]) if n + 1 < len(steps) else None
        qi, j, key0 = step[:3]
        if j == 0 and key0 == 0:
            state = tuple((jnp.full((1, tq), NEG, F32), jnp.zeros((VT_ROWS, tq), F32)) for _ in range(nh))
        state = update(step, scores, state)
        if j == qi and key0:
            out_t = jnp.concatenate([acc[:HEAD_DIM] / acc[HEAD_DIM:HEAD_DIM + 1] for _, acc in state], axis=0)
            o_ref[qi * tq:(qi + 1) * tq, :] = out_t.T.astype(BF16)
        scores = nxt


def _fox_attention(q_aug, k_aug, v_t):
    batch, nt, _, tq = v_t.shape
    assert tq == FOX_TILE
    seq = nt * tq
    nh = FOX_HEADS
    slab = pl.BlockSpec((None, None, seq, nh * LANES), lambda b, hp: (b, hp, 0, 0))
    return pl.pallas_call(
        functools.partial(_fox_kernel, tq=tq, nh=nh),
        grid=(batch, B_HEADS // nh),
        in_specs=[slab, slab,
                  pl.BlockSpec((None, nt, nh * VT_ROWS, tq), lambda b, hp: (b, 0, hp, 0))],
        out_specs=pl.BlockSpec((None, None, seq, nh * HEAD_DIM), lambda b, hp: (b, hp, 0, 0)),
        out_shape=jax.ShapeDtypeStruct((batch, B_HEADS // nh, seq, nh * HEAD_DIM), BF16),
        compiler_params=_params(2),
        name="fox_attn",
    )(q_aug, k_aug, v_t)


def _bias_placement():
    assert 2 * BIAS_PIECES <= BIAS_STRIDE
    place_q = np.zeros((BIAS_PIECES * LANES, LANES), np.float32)
    place_k = np.zeros((BIAS_PIECES * LANES, LANES), np.float32)
    ones_q = np.zeros((1, LANES), np.float32)
    ones_k = np.zeros((1, LANES), np.float32)
    for h in range(B_HEADS):
        base = h * BIAS_STRIDE
        for piece in range(BIAS_PIECES):
            place_q[piece * LANES + h, base + piece] = 1.0
            ones_k[0, base + piece] = 1.0
            place_k[piece * LANES + h, base + BIAS_PIECES + piece] = -1.0
            ones_q[0, base + BIAS_PIECES + piece] = 1.0
    return (jnp.asarray(place_q, BF16), jnp.asarray(ones_q), jnp.asarray(place_k, BF16), jnp.asarray(ones_k))


def _rope_lane_tables(seq):
    pos = jnp.arange(seq, dtype=F32)
    inv = ROPE_THETA ** (-jnp.arange(0, ROPE_DIM, 2, dtype=F32) / ROPE_DIM)
    ang = pos[:, None] * inv[None, :]
    cos, sin = jnp.cos(ang), jnp.sin(ang)
    ones = jnp.ones((seq, HEAD_DIM - ROPE_DIM), F32)
    zeros = jnp.zeros((seq, HEAD_DIM - ROPE_DIM), F32)
    zh = jnp.zeros((seq, ROPE_HALF), F32)
    cos_h = jnp.concatenate([cos, cos, ones], axis=1)
    sa_h = jnp.concatenate([zh, sin, zeros], axis=1)
    sb_h = jnp.concatenate([-sin, zh, zeros], axis=1)
    rep = LANES // HEAD_DIM
    return tuple(jnp.tile(t, (1, rep)) for t in (cos_h, sa_h, sb_h))


def kernel(x, norm_gains, w_qkv_a, w_o_a, w_q_b, w_o_b, kv_norm, w_kvf, b_f, w_up, conv_w, conv_b, w_down):
    batch, seq, _ = x.shape
    m = batch * seq
    cos_t, sa_t, sb_t = _rope_lane_tables(seq)
    place_q, ones_q, place_k, ones_k = _bias_placement()
    gains = norm_gains.reshape(DEPTH * 4, 1, D_MODEL)
    w_qkv_a, w_o_a, w_q_b, w_o_b, w_up, w_down = (
        t.astype(BF16) for t in (w_qkv_a, w_o_a, w_q_b, w_o_b, w_up, w_down))
    w_kv = w_kvf[:, :2 * B_WIDTH].astype(BF16)
    w_f = jnp.pad(w_kvf[:, 2 * B_WIDTH:], ((0, 0), (0, LANES - B_HEADS))).astype(BF16)
    b_f = jnp.pad(b_f, (0, LANES - B_HEADS)).reshape(1, LANES)
    conv_b = conv_b.reshape(DEPTH, 1, 2 * D_FF)

    h = x
    k_aug = v_sh = c_sh = None
    for l in range(DEPTH):
        g = lambda i: gains[4 * l + i]
        if l < N_A:
            qkv = _qkv_proj(h, g(0), w_qkv_a, l, cos_t, sa_t, sb_t)
            mix_inputs, w_o, o_idx = _dilated_attention(qkv), w_o_a, l
        else:
            if l == N_A:
                k_aug, v_sh, c_sh = _kvf_proj(h, kv_norm.reshape(1, D_MODEL), w_kv, w_f, b_f,
                                              place_k, ones_k)
            j = l - N_A
            q_aug = _q_proj(h.reshape(m, D_MODEL), g(0), w_q_b, j, c_sh.reshape(m, LANES),
                            place_q, ones_q, seq)
            mix_inputs, w_o, o_idx = [_fox_attention(q_aug, k_aug, v_sh)], w_o_b, j
        h = _layer_tail(mix_inputs, w_o, o_idx, h, l, gains, w_up, conv_w, conv_b, w_down)
    return h
```

```python
import functools

import numpy as np

import jax
import jax.numpy as jnp
from jax import lax
from jax.experimental import pallas as pl
from jax.experimental.pallas import tpu as pltpu

D_MODEL = 1024
DEPTH = 4
HEAD_DIM = 64
A_GROUPS = ((128, 1), (512, 4), (2048, 16))
A_HEADS_PER_GROUP = 4
A_HEADS = A_HEADS_PER_GROUP * len(A_GROUPS)
A_WIDTH = A_HEADS * HEAD_DIM
GROUP_WIDTH = A_HEADS_PER_GROUP * HEAD_DIM
B_HEADS = D_MODEL // HEAD_DIM
B_WIDTH = B_HEADS * HEAD_DIM
N_A = DEPTH // 2
D_FF = 2816
CONV_W = 3
ROPE_DIM = HEAD_DIM // 4
ROPE_HALF = ROPE_DIM // 2
ROPE_THETA = 500000.0
BLK = 128
EPS = 1e-6
NEG = -1e30
SCALE = HEAD_DIM ** -0.5
LOG2E = float(np.log2(np.e))

LANES = 128
SUBLANES = 8
MXU_DIM = 256
VMEM_LIMIT = 56 * 1024 * 1024

ROW_TILE = 512
FFN_CHUNK = MXU_DIM
TAIL_SUBTILES = 2
FOX_TILE = 512
VT_ROWS = HEAD_DIM + 16
FOX_HEADS = 4
BIAS_PIECES = 3
BIAS_STRIDE = LANES // B_HEADS
DILATED_UNROLL = (3, 4, 4)

F32 = jnp.float32
BF16 = jnp.bfloat16
NT_DIMS = (((1,), (1,)), ((), ()))


def _params(n_axes):
    return pltpu.CompilerParams(dimension_semantics=("arbitrary",) * n_axes,
                                vmem_limit_bytes=VMEM_LIMIT)


def _resident(shape, index_map):
    return pl.BlockSpec(shape, index_map, pipeline_mode=pl.Buffered(1))


def _rms(x, g):
    return x * lax.rsqrt(jnp.mean(x * x, axis=-1, keepdims=True) + EPS) * g


def _qkv_kernel(x_ref, g_ref, w_ref, cos_ref, sa_ref, sb_ref, o0_ref, o1_ref, o2_ref, ybuf_ref, *, tm):
    xn = _rms(x_ref[...], g_ref[...]).astype(BF16)
    y = jnp.dot(xn, w_ref[...], preferred_element_type=F32)
    cos, sa, sb = cos_ref[...], sa_ref[...], sb_ref[...]
    slabs_per_group = 3 * GROUP_WIDTH // LANES
    slabs_per_kind = GROUP_WIDTH // LANES
    for c in range(3 * A_WIDTH // LANES):
        grp, kind = c // slabs_per_group, (c % slabs_per_group) // slabs_per_kind
        src = kind * A_WIDTH + grp * GROUP_WIDTH + (c % slabs_per_kind) * LANES
        t = y[:, src:src + LANES]
        if kind < 2:
            t = t * cos + pltpu.roll(t, ROPE_HALF, 1) * sa + pltpu.roll(t, LANES - ROPE_HALF, 1) * sb
        if kind == 0:
            t = t * SCALE
        if grp == 0:
            o0_ref[0, :, c * LANES:(c + 1) * LANES] = t.astype(BF16)
        else:
            ybuf_ref[c - slabs_per_group] = t
    for grp, o_ref in ((1, o1_ref), (2, o2_ref)):
        r = A_GROUPS[grp][1]
        for j in range(r):
            for c in range(slabs_per_group):
                o_ref[j, :, c * LANES:(c + 1) * LANES] = ybuf_ref[
                    (grp - 1) * slabs_per_group + c, pl.ds(j, tm // r, stride=r), :].astype(BF16)


def _qkv_proj(h3, g, w_all, idx, cos_t, sa_t, sb_t):
    batch, seq, _ = h3.shape
    tm = ROW_TILE
    n = 3 * A_WIDTH
    gw = 3 * GROUP_WIDTH
    tab = pl.BlockSpec((tm, LANES), lambda b, t: (t, 0))
    (_, r1), (_, r2) = A_GROUPS[1], A_GROUPS[2]
    return pl.pallas_call(
        functools.partial(_qkv_kernel, tm=tm),
        grid=(batch, seq // tm),
        in_specs=[pl.BlockSpec((None, tm, D_MODEL), lambda b, t: (b, t, 0)),
                  _resident((1, D_MODEL), lambda b, t: (0, 0)),
                  _resident((None, D_MODEL, n), lambda b, t: (idx, 0, 0)),
                  tab, tab, tab],
        out_specs=[pl.BlockSpec((None, 1, tm, gw), lambda b, t: (b, 0, t, 0)),
                   pl.BlockSpec((None, r1, tm // r1, gw), lambda b, t: (b, 0, t, 0)),
                   pl.BlockSpec((None, r2, tm // r2, gw), lambda b, t: (b, 0, t, 0))],
        out_shape=[jax.ShapeDtypeStruct((batch, 1, seq, gw), BF16),
                   jax.ShapeDtypeStruct((batch, r1, seq // r1, gw), BF16),
                   jax.ShapeDtypeStruct((batch, r2, seq // r2, gw), BF16)],
        scratch_shapes=[pltpu.VMEM((2 * gw // LANES, tm, LANES), F32)],
        compiler_params=_params(2),
        name="qkv_proj",
    )(h3, g, w_all, cos_t, sa_t, sb_t)


def _dilated_kernel(*refs, plans):
    n = len(plans)
    calls = []
    for g, plan in enumerate(plans):
        q_ref, k_ref, v_ref = refs[3 * g:3 * g + 3]
        o_ref, lse_ref = refs[3 * n + 2 * g:3 * n + 2 * g + 2]
        calls.append(_dilated_calls(q_ref, k_ref, v_ref, o_ref, lse_ref, *plan))
    for i in range(max(len(c) for c in calls)):
        for group_calls in calls:
            if i < len(group_calls):
                group_calls[i]()


def _dilated_calls(q_ref, k_ref, v_ref, o_ref, lse_ref, n_res, nb, n_back, unroll):
    nh = A_HEADS_PER_GROUP
    lane = lax.broadcasted_iota(jnp.int32, (BLK, GROUP_WIDTH), 1)
    head_masks = [(lane >= h * HEAD_DIM) & (lane < (h + 1) * HEAD_DIM) for h in range(nh)]

    def band(width):
        qi = lax.broadcasted_iota(jnp.int32, (nh * BLK, width), 0) & (BLK - 1)
        kj = lax.broadcasted_iota(jnp.int32, (nh * BLK, width), 1)
        rel = qi + (width - BLK) - kj
        return (rel >= 0) & (rel <= n_back)

    def attend(items, width):
        mask = band(width)
        scores = []
        for res, q0, k0 in items:
            q = q_ref[res, pl.ds(q0, BLK), :]
            qs = jnp.concatenate([jnp.where(hm, q, jnp.zeros_like(q)) for hm in head_masks], axis=0)
            scores.append(lax.dot_general(qs, k_ref[res, pl.ds(k0, width), :], NT_DIMS,
                                          preferred_element_type=F32))
        probs = []
        for s in scores:
            s = jnp.where(mask, s, NEG)
            m = jnp.max(s, axis=-1, keepdims=True)
            p = jnp.exp(s - m)
            l = jnp.sum(p, axis=-1, keepdims=True)
            probs.append(((p / l).astype(BF16), m + jnp.log(l)))
        for (res, q0, k0), (pn, lse) in zip(items, probs):
            pv = jnp.dot(pn, v_ref[res, pl.ds(k0, width), :], preferred_element_type=F32)
            o = pv[0:BLK]
            ls = jnp.broadcast_to(lse[0:BLK], (BLK, GROUP_WIDTH))
            for h in range(1, nh):
                o = jnp.where(head_masks[h], pv[h * BLK:(h + 1) * BLK], o)
                ls = jnp.where(head_masks[h], lse[h * BLK:(h + 1) * BLK], ls)
            o_ref[res, pl.ds(q0, BLK), :] = o.astype(BF16)
            lse_ref[res, pl.ds(q0, BLK), :] = ls

    firsts = [(res, 0, 0) for res in range(n_res)]
    rest = [(res, i * BLK, (i - 1) * BLK) for i in range(1, nb) for res in range(n_res)]
    return ([functools.partial(attend, firsts[i:i + unroll], BLK) for i in range(0, n_res, unroll)]
            + [functools.partial(attend, rest[i:i + unroll], 2 * BLK) for i in range(0, len(rest), unroll)])


def _dilated_attention(qkvs):
    batch = qkvs[0].shape[0]
    plans, in_specs, out_specs, out_shapes = [], [], [], []
    for (window, r), unroll, qkv_g in zip(A_GROUPS, DILATED_UNROLL, qkvs):
        ln = qkv_g.shape[2]
        assert ln % BLK == 0
        plans.append((r, ln // BLK, window // r, unroll))
        shape = (None, r, ln, GROUP_WIDTH)
        in_specs += [pl.BlockSpec(shape, functools.partial(lambda b, which: (b, 0, 0, which), which=which))
                     for which in range(3)]
        out_specs.append(pl.BlockSpec(shape, lambda b: (b, 0, 0, 0)))
        out_shapes.append((batch, r, ln, GROUP_WIDTH))
    outs = pl.pallas_call(
        functools.partial(_dilated_kernel, plans=tuple(plans)),
        grid=(batch,),
        in_specs=in_specs,
        out_specs=[s for s in out_specs for _ in range(2)],
        out_shape=[jax.ShapeDtypeStruct(s, dt) for s in out_shapes for dt in (BF16, F32)],
        compiler_params=_params(1),
        name="dilated_attn",
    )(*[qkv_g for qkv_g in qkvs for _ in range(3)])
    return list(outs[0::2]) + list(outs[1::2])


def _mix_dilated(o0_ref, o1_ref, o2_ref, l0_ref, l1_ref, l2_ref, w_ref, obuf_ref, lbuf_ref, *, tm, sub):
    slabs = GROUP_WIDTH // LANES
    for grp, (o_ref, l_ref) in enumerate(((o0_ref, l0_ref), (o1_ref, l1_ref), (o2_ref, l2_ref))):
        r = A_GROUPS[grp][1]
        src = slice(sub * tm // r, (sub + 1) * tm // r)
        for j in range(r):
            rows = pl.ds(j, tm // r, stride=r) if r > 1 else slice(None)
            for c in range(slabs):
                cols = slice(c * LANES, (c + 1) * LANES)
                obuf_ref[grp * slabs + c, rows, :] = o_ref[j, src, cols].astype(F32)
                lbuf_ref[grp * slabs + c, rows, :] = l_ref[j, src, cols]

    def group(buf_ref, grp):
        return jnp.concatenate([buf_ref[grp * slabs + c] for c in range(slabs)], axis=1)

    l0, l1, l2 = (group(lbuf_ref, grp) for grp in range(3))
    mx = jnp.maximum(jnp.maximum(l0, l1), l2)
    e0, e1, e2 = jnp.exp(l0 - mx), jnp.exp(l1 - mx), jnp.exp(l2 - mx)
    den = e0 + e1 + e2
    mixed = jnp.concatenate([(group(obuf_ref, grp) * (e / den)).astype(BF16)
                             for grp, e in enumerate((e0, e1, e2))], axis=1)
    return jnp.dot(mixed, w_ref[...], preferred_element_type=F32)


def _to_strided_rows(x, buf_ref):
    tm, d = x.shape
    n = tm // SUBLANES
    pitch = buf_ref.shape[1] // SUBLANES
    slabs = d // LANES
    for c in range(slabs):
        for s in range(SUBLANES):
            buf_ref[c, s * pitch:s * pitch + n, :] = x[s * n:(s + 1) * n, c * LANES:(c + 1) * LANES]
    return jnp.concatenate(
        [jnp.concatenate([buf_ref[c, pl.ds(i, SUBLANES, stride=pitch), :] for c in range(slabs)], axis=1)
         for i in range(n)], axis=0)


def _store_token_rows(y, buf_ref, out_ref, row0):
    tm, d = y.shape
    n = tm // SUBLANES
    per = n // SUBLANES
    slabs = d // LANES
    for c in range(slabs):
        buf_ref[c] = y[:, c * LANES:(c + 1) * LANES]
    for j in range(n):
        start = SUBLANES * SUBLANES * (j % per) + j // per
        out_ref[row0 + j * SUBLANES:row0 + (j + 1) * SUBLANES, :] = jnp.concatenate(
            [buf_ref[c, pl.ds(start, SUBLANES, stride=SUBLANES), :] for c in range(slabs)], axis=1)


def _tail_kernel(*refs, tm, n_mix, n_sub):
    mix_refs, refs = refs[:n_mix + 1], refs[n_mix + 1:]
    h_ref, g1_ref, g2_ref, wup_ref, cw_ref, cb_ref, wdn_ref, g3_ref, out_ref, carry_ref = refs[:10]
    scratch = refs[10:]
    per_sub = 3
    extra = scratch[n_sub * per_sub:]

    @pl.when(pl.program_id(1) == 0)
    def _():
        carry_ref[...] = jnp.zeros_like(carry_ref)

    ends = {}
    subs = [_tail_sub(mix_refs, extra, h_ref, g1_ref, g2_ref, wup_ref, cw_ref, cb_ref, wdn_ref, g3_ref,
                      out_ref, carry_ref, scratch[i * per_sub:(i + 1) * per_sub], ends,
                      tm=tm, sub=i, n_sub=n_sub) for i in range(n_sub)]
    live = []
    while subs or live:
        if subs:
            live.append(subs.pop(0))
        for gen in list(live):
            if next(gen, "done") == "done":
                live.remove(gen)


def _tail_sub(mix_refs, extra, h_ref, g1_ref, g2_ref, wup_ref, cw_ref, cb_ref, wdn_ref, g3_ref, out_ref,
              carry_ref, own, ends, *, tm, sub, n_sub):
    hmid_ref, pin_ref, pout_ref = own
    rows = slice(sub * tm, (sub + 1) * tm)
    if len(mix_refs) == 2:
        o_ref, w_ref = mix_refs
        o = jnp.concatenate([o_ref[grp, rows, :] for grp in range(o_ref.shape[0])], axis=1)
        mix = jnp.dot(o, w_ref[...], preferred_element_type=F32)
    else:
        mix = _mix_dilated(*mix_refs, *extra, tm=tm, sub=sub)
    h = _to_strided_rows(h_ref[rows, :] + _rms(mix, g1_ref[...]), pin_ref)
    xn = _rms(h, g2_ref[...]).astype(BF16)
    last_sublane = lax.broadcasted_iota(jnp.int32, (SUBLANES, FFN_CHUNK), 0) == SUBLANES - 1
    yield

    def conv(col, k):
        cs = slice(col, col + FFN_CHUNK)
        a = jnp.dot(xn, wup_ref[:, cs], preferred_element_type=F32)
        prev = carry_ref[k] if sub == 0 else ends.pop((sub - 1, k))
        if sub == n_sub - 1:
            carry_ref[k] = a[tm - 2 * SUBLANES:tm]
        else:
            ends[(sub, k)] = a[tm - 2 * SUBLANES:tm]
        wrapped = [pltpu.roll(jnp.where(last_sublane, prev[g * SUBLANES:(g + 1) * SUBLANES],
                                        a[tm - (2 - g) * SUBLANES:tm - (1 - g) * SUBLANES]), 1, 0)
                   for g in range(2)]
        a1 = jnp.concatenate([wrapped[1], a[:tm - SUBLANES]], axis=0)
        a2 = jnp.concatenate([wrapped[0], wrapped[1], a[:tm - 2 * SUBLANES]], axis=0)
        return (a2 * cw_ref[0:1, cs] + a1 * cw_ref[1:2, cs] + a * cw_ref[2:3, cs]) + cb_ref[:, cs]

    for c in range(D_FF // FFN_CHUNK):
        gate = conv(c * FFN_CHUNK, 2 * c)
        val = conv(D_FF + c * FFN_CHUNK, 2 * c + 1)
        hmid_ref[:, c * FFN_CHUNK:(c + 1) * FFN_CHUNK] = (
            jax.nn.gelu(gate, approximate=True) * val).astype(BF16)
        yield
    f = jnp.dot(hmid_ref[...], wdn_ref[...], preferred_element_type=F32)
    _store_token_rows(h + _rms(f, g3_ref[...]), pout_ref, out_ref, sub * tm)


def _layer_tail(mix_inputs, w_o_all, o_idx, h3, layer, g_all, w_up, cw, cb, w_down):
    batch, seq, _ = h3.shape
    n_sub = TAIL_SUBTILES
    tm = ROW_TILE // n_sub
    blk = ROW_TILE
    row = pl.BlockSpec((None, blk, D_MODEL), lambda b, t: (b, t, 0))
    gain = lambda i: _resident((None, 1, D_MODEL), lambda b, t: (4 * layer + i, 0, 0))
    if len(mix_inputs) == 1:
        groups, width = mix_inputs[0].shape[1], mix_inputs[0].shape[3]
        mix_specs = [pl.BlockSpec((None, groups, blk, width), lambda b, t: (b, 0, t, 0))]
        extra_scratch = []
    else:
        mix_specs = 2 * [pl.BlockSpec((None, r, blk // r, GROUP_WIDTH), lambda b, t: (b, 0, t, 0))
                         for _, r in A_GROUPS]
        extra_scratch = 2 * [pltpu.VMEM((A_WIDTH // LANES, tm, LANES), F32)]
    return pl.pallas_call(
        functools.partial(_tail_kernel, tm=tm, n_mix=len(mix_inputs), n_sub=n_sub),
        grid=(batch, seq // blk),
        in_specs=mix_specs + [_resident((None,) + w_o_all.shape[1:], lambda b, t: (o_idx, 0, 0)),
                              row, gain(1), gain(2),
                              _resident((None, D_MODEL, 2 * D_FF), lambda b, t: (layer, 0, 0)),
                              _resident((None, CONV_W, 2 * D_FF), lambda b, t: (layer, 0, 0)),
                              _resident((None, 1, 2 * D_FF), lambda b, t: (layer, 0, 0)),
                              _resident((None, D_FF, D_MODEL), lambda b, t: (layer, 0, 0)),
                              gain(3)],
        out_specs=row,
        out_shape=jax.ShapeDtypeStruct(h3.shape, F32),
        scratch_shapes=([pltpu.VMEM((2 * D_FF // FFN_CHUNK, 2 * SUBLANES, FFN_CHUNK), F32)]
                        + n_sub * ([pltpu.VMEM((tm, D_FF), BF16)]
                                   + [pltpu.VMEM((D_MODEL // LANES, tm + SUBLANES * SUBLANES, LANES), F32),
                                      pltpu.VMEM((D_MODEL // LANES, tm, LANES), F32)])
                        + extra_scratch),
        compiler_params=_params(2),
        name="layer_tail",
    )(*mix_inputs, w_o_all, h3, g_all, g_all, w_up, cw, cb, w_down, g_all)


def _split_bf16(c):
    pieces, rest = [], c
    for _ in range(BIAS_PIECES):
        piece = rest.astype(BF16)
        pieces.append(piece)
        rest = rest - piece.astype(F32)
    return jnp.concatenate(pieces, axis=1)


def _store_augmented(project, scale, c_tile, place_ref, ones_ref, out_ref):
    tm = c_tile.shape[0]
    slab = jnp.dot(_split_bf16(c_tile * LOG2E), place_ref[...], preferred_element_type=F32) + ones_ref[...]
    lane = lax.broadcasted_iota(jnp.int32, (tm, LANES), 1)
    own = lane < HEAD_DIM
    bias_lanes = lane < HEAD_DIM + 2 * BIAS_PIECES
    heads_per_piece = MXU_DIM // HEAD_DIM
    for h in range(B_HEADS):
        if h % heads_per_piece == 0:
            x = project(slice(h * HEAD_DIM, h * HEAD_DIM + MXU_DIM)) * scale
        hp = h % heads_per_piece
        pair = x[:, (hp // 2) * LANES:(hp // 2 + 1) * LANES]
        if h % 2 == 1:
            pair = pltpu.roll(pair, HEAD_DIM, 1)
        shift = (HEAD_DIM - BIAS_STRIDE * h) % LANES
        bias = pltpu.roll(slab, shift, 1) if shift else slab
        lane0 = (h % FOX_HEADS) * LANES
        out_ref[h // FOX_HEADS, :, lane0:lane0 + LANES] = jnp.where(
            own, pair, jnp.where(bias_lanes, bias, 0.0)).astype(BF16)


def _kvf_kernel(x_ref, g_ref, wkv_ref, wf_ref, bf_ref, place_ref, ones_ref, gq_ref, wq_ref, placeq_ref,
                onesq_ref, k_ref, v_ref, c_ref, q_ref, carry_ref, *, tm):
    @pl.when(pl.program_id(1) == 0)
    def _():
        carry_ref[...] = jnp.zeros_like(carry_ref)

    x = x_ref[...]
    y = x * lax.rsqrt(jnp.mean(x * x, axis=-1, keepdims=True) + EPS)
    xn = (y * g_ref[...]).astype(BF16)
    xq = (y * gq_ref[...]).astype(BF16)
    z = jnp.dot(xn, wf_ref[...], preferred_element_type=F32) + bf_ref[...]
    log_f = -(jnp.maximum(-z, 0.0) + jnp.log1p(jnp.exp(-jnp.abs(z))))
    rows = lax.broadcasted_iota(jnp.int32, (tm, LANES), 0)
    acc = log_f
    shift = 1
    while shift < tm:
        acc = acc + jnp.where(rows >= shift, pltpu.roll(acc, shift, 0), 0.0)
        shift *= 2
    acc = acc + carry_ref[...]
    c_ref[...] = acc
    carry_ref[...] = acc[tm - 1:tm, :]
    _store_augmented(lambda cols: jnp.dot(xn, wkv_ref[:, cols], preferred_element_type=F32), 1.0,
                     acc, place_ref, ones_ref, k_ref)
    pad_rows = lax.broadcasted_iota(jnp.int32, (VT_ROWS - HEAD_DIM, tm), 0)
    ones_pad = jnp.where(pad_rows == 0, 1.0, 0.0)
    heads_per_piece = MXU_DIM // HEAD_DIM
    for h0 in range(0, B_HEADS, heads_per_piece):
        cols = slice(B_WIDTH + h0 * HEAD_DIM, B_WIDTH + h0 * HEAD_DIM + MXU_DIM)
        vt = jnp.dot(xn, wkv_ref[:, cols], preferred_element_type=F32).T
        for hp in range(heads_per_piece):
            h = h0 + hp
            v_ref[h * VT_ROWS:(h + 1) * VT_ROWS, :] = jnp.concatenate(
                [vt[hp * HEAD_DIM:(hp + 1) * HEAD_DIM], ones_pad], axis=0).astype(BF16)
    _store_augmented(lambda cols: jnp.dot(xq, wq_ref[:, cols], preferred_element_type=F32), SCALE * LOG2E,
                     acc, placeq_ref, onesq_ref, q_ref)


def _kvf_proj(h3, g, wkv, wf, bf, place_k, ones_k, gq, wq_all, q_idx, place_q, ones_q):
    batch, seq, _ = h3.shape
    tm = ROW_TILE
    row = lambda width: pl.BlockSpec((None, tm, width), lambda b, t: (b, t, 0))
    aug = pl.BlockSpec((None, B_HEADS // FOX_HEADS, tm, FOX_HEADS * LANES), lambda b, t: (b, 0, t, 0))
    aug_shape = jax.ShapeDtypeStruct((batch, B_HEADS // FOX_HEADS, seq, FOX_HEADS * LANES), BF16)
    return pl.pallas_call(
        functools.partial(_kvf_kernel, tm=tm),
        grid=(batch, seq // tm),
        in_specs=[row(D_MODEL),
                  _resident((1, D_MODEL), lambda b, t: (0, 0)),
                  _resident((D_MODEL, 2 * B_WIDTH), lambda b, t: (0, 0)),
                  _resident((D_MODEL, LANES), lambda b, t: (0, 0)),
                  _resident((1, LANES), lambda b, t: (0, 0)),
                  _resident((BIAS_PIECES * LANES, LANES), lambda b, t: (0, 0)),
                  _resident((1, LANES), lambda b, t: (0, 0)),
                  _resident((1, D_MODEL), lambda b, t: (0, 0)),
                  _resident((None, D_MODEL, B_WIDTH), lambda b, t: (q_idx, 0, 0)),
                  _resident((BIAS_PIECES * LANES, LANES), lambda b, t: (0, 0)),
                  _resident((1, LANES), lambda b, t: (0, 0))],
        out_specs=[aug,
                   pl.BlockSpec((None, None, B_HEADS * VT_ROWS, tm), lambda b, t: (b, t, 0, 0)),
                   row(LANES), aug],
        out_shape=[aug_shape,
                   jax.ShapeDtypeStruct((batch, seq // tm, B_HEADS * VT_ROWS, tm), BF16),
                   jax.ShapeDtypeStruct((batch, seq, LANES), F32), aug_shape],
        scratch_shapes=[pltpu.VMEM((1, LANES), F32)],
        compiler_params=_params(2),
        name="kvf_proj",
    )(h3, g, wkv, wf, bf, place_k, ones_k, gq, wq_all, place_q, ones_q)


def _qproj_kernel(x_ref, g_ref, w_ref, c_ref, place_ref, ones_ref, o_ref):
    xn = _rms(x_ref[...], g_ref[...]).astype(BF16)
    _store_augmented(lambda cols: jnp.dot(xn, w_ref[:, cols], preferred_element_type=F32), SCALE * LOG2E,
                     c_ref[...], place_ref, ones_ref, o_ref)


def _q_proj(h2, g, w_all, idx, c2, place_q, ones_q, seq):
    m = h2.shape[0]
    tm = ROW_TILE
    nt = seq // tm
    groups = B_HEADS // FOX_HEADS
    return pl.pallas_call(
        _qproj_kernel,
        grid=(m // tm,),
        in_specs=[pl.BlockSpec((tm, D_MODEL), lambda i: (i, 0)),
                  _resident((1, D_MODEL), lambda i: (0, 0)),
                  _resident((None, D_MODEL, B_WIDTH), lambda i: (idx, 0, 0)),
                  pl.BlockSpec((tm, LANES), lambda i: (i, 0)),
                  _resident((BIAS_PIECES * LANES, LANES), lambda i: (0, 0)),
                  _resident((1, LANES), lambda i: (0, 0))],
        out_specs=pl.BlockSpec((None, groups, tm, FOX_HEADS * LANES), lambda i: (i // nt, 0, i % nt, 0)),
        out_shape=jax.ShapeDtypeStruct((m // seq, groups, seq, FOX_HEADS * LANES), BF16),
        compiler_params=_params(1),
        name="q_proj",
    )(h2, g, w_all, c2, place_q, ones_q)


def _fox_kernel(q_ref, k_ref, vt_ref, o_ref, *, tq, nh):
    half = tq // 2

    def scores_of(qi, j, key0, nkeys, query0, diagonal):
        return [lax.dot_general(k_ref[j * tq + key0:j * tq + key0 + nkeys, hh * LANES:(hh + 1) * LANES],
                                q_ref[qi * tq + query0:(qi + 1) * tq, hh * LANES:(hh + 1) * LANES],
                                NT_DIMS, preferred_element_type=F32) for hh in range(nh)]

    def update(step, scores, state):
        _, j, key0, nkeys, query0, diagonal = step
        stats = []
        for hh in range(nh):
            m = state[hh][0][:, query0:]
            s = scores[hh]
            if diagonal:
                visible = (lax.broadcasted_iota(jnp.int32, s.shape, 0)
                           <= lax.broadcasted_iota(jnp.int32, s.shape, 1))
                s = jnp.where(visible, s, NEG)
            m_new = jnp.maximum(m, jnp.max(s, axis=0, keepdims=True))
            stats.append((m_new, jnp.exp2(m - m_new), jnp.exp2(s - m_new).astype(BF16)))
        out = []
        for hh in range(nh):
            m_new, alpha, p = stats[hh]
            vt = vt_ref[j, hh * VT_ROWS:(hh + 1) * VT_ROWS, key0:key0 + nkeys]
            m_old, acc_old = state[hh]
            acc = alpha * acc_old[:, query0:] + jnp.dot(vt, p, preferred_element_type=F32)
            if query0:
                m_new = jnp.concatenate([m_old[:, :query0], m_new], axis=1)
                acc = jnp.concatenate([acc_old[:, :query0], acc], axis=1)
            out.append((m_new, acc))
        return tuple(out)

    steps = []
    for qi in range(q_ref.shape[0] // tq):
        steps += [(qi, j, 0, tq, 0, False) for j in range(qi)]
        steps += [(qi, qi, 0, half, 0, True), (qi, qi, half, half, half, True)]
    scores = scores_of(*steps[0])
    state = None
    for n, step in enumerate(steps):
        nxt = scores_of(*steps[n + 1]) if n + 1 < len(steps) else None
        qi, j, key0 = step[:3]
        if j == 0 and key0 == 0:
            state = tuple((jnp.full((1, tq), NEG, F32), jnp.zeros((VT_ROWS, tq), F32)) for _ in range(nh))
        state = update(step, scores, state)
        if j == qi and key0:
            out_t = jnp.concatenate([acc[:HEAD_DIM] / acc[HEAD_DIM:HEAD_DIM + 1] for _, acc in state], axis=0)
            o_ref[qi * tq:(qi + 1) * tq, :] = out_t.T.astype(BF16)
        scores = nxt


def _fox_attention(q_aug, k_aug, v_t):
    batch, nt, _, tq = v_t.shape
    assert tq == FOX_TILE
    seq = nt * tq
    nh = FOX_HEADS
    slab = pl.BlockSpec((None, None, seq, nh * LANES), lambda b, hp: (b, hp, 0, 0))
    return pl.pallas_call(
        functools.partial(_fox_kernel, tq=tq, nh=nh),
        grid=(batch, B_HEADS // nh),
        in_specs=[slab, slab,
                  pl.BlockSpec((None, nt, nh * VT_ROWS, tq), lambda b, hp: (b, 0, hp, 0))],
        out_specs=pl.BlockSpec((None, None, seq, nh * HEAD_DIM), lambda b, hp: (b, hp, 0, 0)),
        out_shape=jax.ShapeDtypeStruct((batch, B_HEADS // nh, seq, nh * HEAD_DIM), BF16),
        compiler_params=_params(2),
        name="fox_attn",
    )(q_aug, k_aug, v_t)


def _bias_placement():
    assert 2 * BIAS_PIECES <= BIAS_STRIDE
    place_q = np.zeros((BIAS_PIECES * LANES, LANES), np.float32)
    place_k = np.zeros((BIAS_PIECES * LANES, LANES), np.float32)
    ones_q = np.zeros((1, LANES), np.float32)
    ones_k = np.zeros((1, LANES), np.float32)
    for h in range(B_HEADS):
        base = h * BIAS_STRIDE
        for piece in range(BIAS_PIECES):
            place_q[piece * LANES + h, base + piece] = 1.0
            ones_k[0, base + piece] = 1.0
            place_k[piece * LANES + h, base + BIAS_PIECES + piece] = -1.0
            ones_q[0, base + BIAS_PIECES + piece] = 1.0
    return (jnp.asarray(place_q, BF16), jnp.asarray(ones_q), jnp.asarray(place_k, BF16), jnp.asarray(ones_k))


def _rope_lane_tables(seq):
    pos = jnp.arange(seq, dtype=F32)
    inv = ROPE_THETA ** (-jnp.arange(0, ROPE_DIM, 2, dtype=F32) / ROPE_DIM)
    ang = pos[:, None] * inv[None, :]
    cos, sin = jnp.cos(ang), jnp.sin(ang)
    ones = jnp.ones((seq, HEAD_DIM - ROPE_DIM), F32)
    zeros = jnp.zeros((seq, HEAD_DIM - ROPE_DIM), F32)
    zh = jnp.zeros((seq, ROPE_HALF), F32)
    cos_h = jnp.concatenate([cos, cos, ones], axis=1)
    sa_h = jnp.concatenate([zh, sin, zeros], axis=1)
    sb_h = jnp.concatenate([-sin, zh, zeros], axis=1)
    rep = LANES // HEAD_DIM
    return tuple(jnp.tile(t, (1, rep)) for t in (cos_h, sa_h, sb_h))


def kernel(x, norm_gains, w_qkv_a, w_o_a, w_q_b, w_o_b, kv_norm, w_kvf, b_f, w_up, conv_w, conv_b, w_down):
    batch, seq, _ = x.shape
    m = batch * seq
    cos_t, sa_t, sb_t = _rope_lane_tables(seq)
    place_q, ones_q, place_k, ones_k = _bias_placement()
    gains = norm_gains.reshape(DEPTH * 4, 1, D_MODEL)
    w_qkv_a, w_o_a, w_q_b, w_o_b, w_up, w_down = (
        t.astype(BF16) for t in (w_qkv_a, w_o_a, w_q_b, w_o_b, w_up, w_down))
    w_kv = w_kvf[:, :2 * B_WIDTH].astype(BF16)
    w_f = jnp.pad(w_kvf[:, 2 * B_WIDTH:], ((0, 0), (0, LANES - B_HEADS))).astype(BF16)
    b_f = jnp.pad(b_f, (0, LANES - B_HEADS)).reshape(1, LANES)
    conv_b = conv_b.reshape(DEPTH, 1, 2 * D_FF)

    h = x
    k_aug = v_sh = c_sh = None
    for l in range(DEPTH):
        g = lambda i: gains[4 * l + i]
        if l < N_A:
            qkv = _qkv_proj(h, g(0), w_qkv_a, l, cos_t, sa_t, sb_t)
            mix_inputs, w_o, o_idx = _dilated_attention(qkv), w_o_a, l
        else:
            j = l - N_A
            if l == N_A:
                k_aug, v_sh, c_sh, q_aug = _kvf_proj(h, kv_norm.reshape(1, D_MODEL), w_kv, w_f, b_f,
                                                     place_k, ones_k, g(0), w_q_b, j, place_q, ones_q)
            else:
                q_aug = _q_proj(h.reshape(m, D_MODEL), g(0), w_q_b, j, c_sh.reshape(m, LANES),
                                place_q, ones_q, seq)
            mix_inputs, w_o, o_idx = [_fox_attention(q_aug, k_aug, v_sh)], w_o_b, j
        h = _layer_tail(mix_inputs, w_o, o_idx, h, l, gains, w_up, conv_w, conv_b, w_down)
    return h
```

```python
import functools

import numpy as np

import jax
import jax.numpy as jnp
from jax import lax
from jax.experimental import pallas as pl
from jax.experimental.pallas import tpu as pltpu

D_MODEL = 1024
DEPTH = 4
HEAD_DIM = 64
A_GROUPS = ((128, 1), (512, 4), (2048, 16))
A_HEADS_PER_GROUP = 4
A_HEADS = A_HEADS_PER_GROUP * len(A_GROUPS)
A_WIDTH = A_HEADS * HEAD_DIM
GROUP_WIDTH = A_HEADS_PER_GROUP * HEAD_DIM
B_HEADS = D_MODEL // HEAD_DIM
B_WIDTH = B_HEADS * HEAD_DIM
N_A = DEPTH // 2
D_FF = 2816
CONV_W = 3
ROPE_DIM = HEAD_DIM // 4
ROPE_HALF = ROPE_DIM // 2
ROPE_THETA = 500000.0
BLK = 128
EPS = 1e-6
NEG = -1e30
SCALE = HEAD_DIM ** -0.5
LOG2E = float(np.log2(np.e))

LANES = 128
SUBLANES = 8
MXU_DIM = 256
VMEM_LIMIT = 56 * 1024 * 1024

ROW_TILE = 512
FFN_CHUNK = MXU_DIM
TAIL_SUBTILES = 2
FOX_TILE = 512
VT_ROWS = HEAD_DIM + 16
FOX_HEADS = 4
BIAS_PIECES = 3
BIAS_STRIDE = LANES // B_HEADS
DILATED_UNROLL = (3, 4, 4)

F32 = jnp.float32
BF16 = jnp.bfloat16
NT_DIMS = (((1,), (1,)), ((), ()))


def _params(n_axes):
    return pltpu.CompilerParams(dimension_semantics=("arbitrary",) * n_axes,
                                vmem_limit_bytes=VMEM_LIMIT)


def _resident(shape, index_map):
    return pl.BlockSpec(shape, index_map, pipeline_mode=pl.Buffered(1))


def _rms(x, g):
    return x * lax.rsqrt(jnp.mean(x * x, axis=-1, keepdims=True) + EPS) * g


def _qkv_kernel(x_ref, g_ref, w_ref, cos_ref, sa_ref, sb_ref, o0_ref, o1_ref, o2_ref, ybuf_ref, *, tm):
    xn = _rms(x_ref[...], g_ref[...]).astype(BF16)
    y = jnp.dot(xn, w_ref[...], preferred_element_type=F32)
    cos, sa, sb = cos_ref[...], sa_ref[...], sb_ref[...]
    slabs_per_group = 3 * GROUP_WIDTH // LANES
    slabs_per_kind = GROUP_WIDTH // LANES
    for c in range(3 * A_WIDTH // LANES):
        grp, kind = c // slabs_per_group, (c % slabs_per_group) // slabs_per_kind
        src = kind * A_WIDTH + grp * GROUP_WIDTH + (c % slabs_per_kind) * LANES
        t = y[:, src:src + LANES]
        if kind < 2:
            t = t * cos + pltpu.roll(t, ROPE_HALF, 1) * sa + pltpu.roll(t, LANES - ROPE_HALF, 1) * sb
        if kind == 0:
            t = t * SCALE
        if grp == 0:
            o0_ref[0, :, c * LANES:(c + 1) * LANES] = t.astype(BF16)
        else:
            ybuf_ref[c - slabs_per_group] = t
    for grp, o_ref in ((1, o1_ref), (2, o2_ref)):
        r = A_GROUPS[grp][1]
        for j in range(r):
            for c in range(slabs_per_group):
                o_ref[j, :, c * LANES:(c + 1) * LANES] = ybuf_ref[
                    (grp - 1) * slabs_per_group + c, pl.ds(j, tm // r, stride=r), :].astype(BF16)


def _qkv_proj(h3, g, w_all, idx, cos_t, sa_t, sb_t):
    batch, seq, _ = h3.shape
    tm = ROW_TILE
    n = 3 * A_WIDTH
    gw = 3 * GROUP_WIDTH
    tab = pl.BlockSpec((tm, LANES), lambda b, t: (t, 0))
    (_, r1), (_, r2) = A_GROUPS[1], A_GROUPS[2]
    return pl.pallas_call(
        functools.partial(_qkv_kernel, tm=tm),
        grid=(batch, seq // tm),
        in_specs=[pl.BlockSpec((None, tm, D_MODEL), lambda b, t: (b, t, 0)),
                  _resident((1, D_MODEL), lambda b, t: (0, 0)),
                  _resident((None, D_MODEL, n), lambda b, t: (idx, 0, 0)),
                  tab, tab, tab],
        out_specs=[pl.BlockSpec((None, 1, tm, gw), lambda b, t: (b, 0, t, 0)),
                   pl.BlockSpec((None, r1, tm // r1, gw), lambda b, t: (b, 0, t, 0)),
                   pl.BlockSpec((None, r2, tm // r2, gw), lambda b, t: (b, 0, t, 0))],
        out_shape=[jax.ShapeDtypeStruct((batch, 1, seq, gw), BF16),
                   jax.ShapeDtypeStruct((batch, r1, seq // r1, gw), BF16),
                   jax.ShapeDtypeStruct((batch, r2, seq // r2, gw), BF16)],
        scratch_shapes=[pltpu.VMEM((2 * gw // LANES, tm, LANES), F32)],
        compiler_params=_params(2),
        name="qkv_proj",
    )(h3, g, w_all, cos_t, sa_t, sb_t)


def _dilated_kernel(*refs, plans):
    n = len(plans)
    calls = []
    for g, plan in enumerate(plans):
        q_ref, k_ref, v_ref = refs[3 * g:3 * g + 3]
        o_ref, lse_ref = refs[3 * n + 2 * g:3 * n + 2 * g + 2]
        calls.append(_dilated_calls(q_ref, k_ref, v_ref, o_ref, lse_ref, *plan))
    for i in range(max(len(c) for c in calls)):
        for group_calls in calls:
            if i < len(group_calls):
                group_calls[i]()


def _dilated_calls(q_ref, k_ref, v_ref, o_ref, lse_ref, n_res, nb, n_back, unroll):
    nh = A_HEADS_PER_GROUP
    lane = lax.broadcasted_iota(jnp.int32, (BLK, GROUP_WIDTH), 1)
    head_masks = [(lane >= h * HEAD_DIM) & (lane < (h + 1) * HEAD_DIM) for h in range(nh)]

    def band(width):
        qi = lax.broadcasted_iota(jnp.int32, (nh * BLK, width), 0) & (BLK - 1)
        kj = lax.broadcasted_iota(jnp.int32, (nh * BLK, width), 1)
        rel = qi + (width - BLK) - kj
        return (rel >= 0) & (rel <= n_back)

    def attend(items, width):
        mask = band(width)
        scores = []
        for res, q0, k0 in items:
            q = q_ref[res, pl.ds(q0, BLK), :]
            qs = jnp.concatenate([jnp.where(hm, q, jnp.zeros_like(q)) for hm in head_masks], axis=0)
            scores.append(lax.dot_general(qs, k_ref[res, pl.ds(k0, width), :], NT_DIMS,
                                          preferred_element_type=F32))
        probs = []
        for s in scores:
            s = jnp.where(mask, s, NEG)
            m = jnp.max(s, axis=-1, keepdims=True)
            p = jnp.exp(s - m)
            l = jnp.sum(p, axis=-1, keepdims=True)
            probs.append(((p / l).astype(BF16), m + jnp.log(l)))
        for (res, q0, k0), (pn, lse) in zip(items, probs):
            pv = jnp.dot(pn, v_ref[res, pl.ds(k0, width), :], preferred_element_type=F32)
            o = pv[0:BLK]
            ls = jnp.broadcast_to(lse[0:BLK], (BLK, GROUP_WIDTH))
            for h in range(1, nh):
                o = jnp.where(head_masks[h], pv[h * BLK:(h + 1) * BLK], o)
                ls = jnp.where(head_masks[h], lse[h * BLK:(h + 1) * BLK], ls)
            o_ref[res, pl.ds(q0, BLK), :] = o.astype(BF16)
            lse_ref[res, pl.ds(q0, BLK), :] = ls

    firsts = [(res, 0, 0) for res in range(n_res)]
    rest = [(res, i * BLK, (i - 1) * BLK) for i in range(1, nb) for res in range(n_res)]
    return ([functools.partial(attend, firsts[i:i + unroll], BLK) for i in range(0, n_res, unroll)]
            + [functools.partial(attend, rest[i:i + unroll], 2 * BLK) for i in range(0, len(rest), unroll)])


def _dilated_attention(qkvs):
    batch = qkvs[0].shape[0]
    plans, in_specs, out_specs, out_shapes = [], [], [], []
    for (window, r), unroll, qkv_g in zip(A_GROUPS, DILATED_UNROLL, qkvs):
        ln = qkv_g.shape[2]
        assert ln % BLK == 0
        plans.append((r, ln // BLK, window // r, unroll))
        shape = (None, r, ln, GROUP_WIDTH)
        in_specs += [pl.BlockSpec(shape, functools.partial(lambda b, which: (b, 0, 0, which), which=which))
                     for which in range(3)]
        out_specs.append(pl.BlockSpec(shape, lambda b: (b, 0, 0, 0)))
        out_shapes.append((batch, r, ln, GROUP_WIDTH))
    outs = pl.pallas_call(
        functools.partial(_dilated_kernel, plans=tuple(plans)),
        grid=(batch,),
        in_specs=in_specs,
        out_specs=[s for s in out_specs for _ in range(2)],
        out_shape=[jax.ShapeDtypeStruct(s, dt) for s in out_shapes for dt in (BF16, F32)],
        compiler_params=_params(1),
        name="dilated_attn",
    )(*[qkv_g for qkv_g in qkvs for _ in range(3)])
    return list(outs[0::2]) + list(outs[1::2])


def _mix_dilated(o0_ref, o1_ref, o2_ref, l0_ref, l1_ref, l2_ref, w_ref, obuf_ref, lbuf_ref, *, tm, sub):
    slabs = GROUP_WIDTH // LANES
    for grp, (o_ref, l_ref) in enumerate(((o0_ref, l0_ref), (o1_ref, l1_ref), (o2_ref, l2_ref))):
        r = A_GROUPS[grp][1]
        src = slice(sub * tm // r, (sub + 1) * tm // r)
        for j in range(r):
            rows = pl.ds(j, tm // r, stride=r) if r > 1 else slice(None)
            for c in range(slabs):
                cols = slice(c * LANES, (c + 1) * LANES)
                obuf_ref[grp * slabs + c, rows, :] = o_ref[j, src, cols].astype(F32)
                lbuf_ref[grp * slabs + c, rows, :] = l_ref[j, src, cols]

    def group(buf_ref, grp):
        return jnp.concatenate([buf_ref[grp * slabs + c] for c in range(slabs)], axis=1)

    l0, l1, l2 = (group(lbuf_ref, grp) for grp in range(3))
    mx = jnp.maximum(jnp.maximum(l0, l1), l2)
    e0, e1, e2 = jnp.exp(l0 - mx), jnp.exp(l1 - mx), jnp.exp(l2 - mx)
    den = e0 + e1 + e2
    mixed = jnp.concatenate([(group(obuf_ref, grp) * (e / den)).astype(BF16)
                             for grp, e in enumerate((e0, e1, e2))], axis=1)
    return jnp.dot(mixed, w_ref[...], preferred_element_type=F32)


def _to_strided_rows(x, buf_ref):
    tm, d = x.shape
    n = tm // SUBLANES
    pitch = buf_ref.shape[1] // SUBLANES
    slabs = d // LANES
    for c in range(slabs):
        for s in range(SUBLANES):
            buf_ref[c, s * pitch:s * pitch + n, :] = x[s * n:(s + 1) * n, c * LANES:(c + 1) * LANES]
    return jnp.concatenate(
        [jnp.concatenate([buf_ref[c, pl.ds(i, SUBLANES, stride=pitch), :] for c in range(slabs)], axis=1)
         for i in range(n)], axis=0)


def _store_token_rows(y, buf_ref, out_ref, row0):
    tm, d = y.shape
    n = tm // SUBLANES
    per = n // SUBLANES
    slabs = d // LANES
    for c in range(slabs):
        buf_ref[c] = y[:, c * LANES:(c + 1) * LANES]
    for j in range(n):
        start = SUBLANES * SUBLANES * (j % per) + j // per
        out_ref[row0 + j * SUBLANES:row0 + (j + 1) * SUBLANES, :] = jnp.concatenate(
            [buf_ref[c, pl.ds(start, SUBLANES, stride=SUBLANES), :] for c in range(slabs)], axis=1)


def _tail_kernel(*refs, tm, n_mix, n_sub):
    mix_refs, refs = refs[:n_mix + 1], refs[n_mix + 1:]
    h_ref, g1_ref, g2_ref, wup_ref, cw_ref, cb_ref, wdn_ref, g3_ref, out_ref, carry_ref = refs[:10]
    scratch = refs[10:]
    per_sub = 3
    extra = scratch[n_sub * per_sub:]

    @pl.when(pl.program_id(1) == 0)
    def _():
        carry_ref[...] = jnp.zeros_like(carry_ref)

    ends = {}
    subs = [_tail_sub(mix_refs, extra, h_ref, g1_ref, g2_ref, wup_ref, cw_ref, cb_ref, wdn_ref, g3_ref,
                      out_ref, carry_ref, scratch[i * per_sub:(i + 1) * per_sub], ends,
                      tm=tm, sub=i, n_sub=n_sub) for i in range(n_sub)]
    live = []
    while subs or live:
        if subs:
            live.append(subs.pop(0))
        for gen in list(live):
            if next(gen, "done") == "done":
                live.remove(gen)


def _tail_sub(mix_refs, extra, h_ref, g1_ref, g2_ref, wup_ref, cw_ref, cb_ref, wdn_ref, g3_ref, out_ref,
              carry_ref, own, ends, *, tm, sub, n_sub):
    hmid_ref, pin_ref, pout_ref = own
    rows = slice(sub * tm, (sub + 1) * tm)
    if len(mix_refs) == 2:
        o_ref, w_ref = mix_refs
        o = jnp.concatenate([o_ref[grp, rows, :] for grp in range(o_ref.shape[0])], axis=1)
        mix = jnp.dot(o, w_ref[...], preferred_element_type=F32)
    else:
        mix = _mix_dilated(*mix_refs, *extra, tm=tm, sub=sub)
    h = _to_strided_rows(h_ref[rows, :] + _rms(mix, g1_ref[...]), pin_ref)
    xn = _rms(h, g2_ref[...]).astype(BF16)
    last_sublane = lax.broadcasted_iota(jnp.int32, (SUBLANES, FFN_CHUNK), 0) == SUBLANES - 1
    yield

    def conv(col, k):
        cs = slice(col, col + FFN_CHUNK)
        a = jnp.dot(xn, wup_ref[:, cs], preferred_element_type=F32)
        prev = carry_ref[k] if sub == 0 else ends.pop((sub - 1, k))
        if sub == n_sub - 1:
            carry_ref[k] = a[tm - 2 * SUBLANES:tm]
        else:
            ends[(sub, k)] = a[tm - 2 * SUBLANES:tm]
        wrapped = [pltpu.roll(jnp.where(last_sublane, prev[g * SUBLANES:(g + 1) * SUBLANES],
                                        a[tm - (2 - g) * SUBLANES:tm - (1 - g) * SUBLANES]), 1, 0)
                   for g in range(2)]
        a1 = jnp.concatenate([wrapped[1], a[:tm - SUBLANES]], axis=0)
        a2 = jnp.concatenate([wrapped[0], wrapped[1], a[:tm - 2 * SUBLANES]], axis=0)
        return (a2 * cw_ref[0:1, cs] + a1 * cw_ref[1:2, cs] + a * cw_ref[2:3, cs]) + cb_ref[:, cs]

    for c in range(D_FF // FFN_CHUNK):
        gate = conv(c * FFN_CHUNK, 2 * c)
        val = conv(D_FF + c * FFN_CHUNK, 2 * c + 1)
        hmid_ref[:, c * FFN_CHUNK:(c + 1) * FFN_CHUNK] = (
            jax.nn.gelu(gate, approximate=True) * val).astype(BF16)
        yield
    f = jnp.dot(hmid_ref[...], wdn_ref[...], preferred_element_type=F32)
    _store_token_rows(h + _rms(f, g3_ref[...]), pout_ref, out_ref, sub * tm)


def _layer_tail(mix_inputs, w_o_all, o_idx, h3, layer, g_all, w_up, cw, cb, w_down):
    batch, seq, _ = h3.shape
    n_sub = TAIL_SUBTILES
    tm = ROW_TILE // n_sub
    blk = ROW_TILE
    row = pl.BlockSpec((None, blk, D_MODEL), lambda b, t: (b, t, 0))
    gain = lambda i: _resident((None, 1, D_MODEL), lambda b, t: (4 * layer + i, 0, 0))
    if len(mix_inputs) == 1:
        groups, width = mix_inputs[0].shape[1], mix_inputs[0].shape[3]
        mix_specs = [pl.BlockSpec((None, groups, blk, width), lambda b, t: (b, 0, t, 0))]
        extra_scratch = []
    else:
        mix_specs = 2 * [pl.BlockSpec((None, r, blk // r, GROUP_WIDTH), lambda b, t: (b, 0, t, 0))
                         for _, r in A_GROUPS]
        extra_scratch = 2 * [pltpu.VMEM((A_WIDTH // LANES, tm, LANES), F32)]
    return pl.pallas_call(
        functools.partial(_tail_kernel, tm=tm, n_mix=len(mix_inputs), n_sub=n_sub),
        grid=(batch, seq // blk),
        in_specs=mix_specs + [_resident((None,) + w_o_all.shape[1:], lambda b, t: (o_idx, 0, 0)),
                              row, gain(1), gain(2),
                              _resident((None, D_MODEL, 2 * D_FF), lambda b, t: (layer, 0, 0)),
                              _resident((None, CONV_W, 2 * D_FF), lambda b, t: (layer, 0, 0)),
                              _resident((None, 1, 2 * D_FF), lambda b, t: (layer, 0, 0)),
                              _resident((None, D_FF, D_MODEL), lambda b, t: (layer, 0, 0)),
                              gain(3)],
        out_specs=row,
        out_shape=jax.ShapeDtypeStruct(h3.shape, F32),
        scratch_shapes=([pltpu.VMEM((2 * D_FF // FFN_CHUNK, 2 * SUBLANES, FFN_CHUNK), F32)]
                        + n_sub * ([pltpu.VMEM((tm, D_FF), BF16)]
                                   + [pltpu.VMEM((D_MODEL // LANES, tm + SUBLANES * SUBLANES, LANES), F32),
                                      pltpu.VMEM((D_MODEL // LANES, tm, LANES), F32)])
                        + extra_scratch),
        compiler_params=_params(2),
        name="layer_tail",
    )(*mix_inputs, w_o_all, h3, g_all, g_all, w_up, cw, cb, w_down, g_all)


def _split_bf16(c):
    pieces, rest = [], c
    for _ in range(BIAS_PIECES):
        piece = rest.astype(BF16)
        pieces.append(piece)
        rest = rest - piece.astype(F32)
    return jnp.concatenate(pieces, axis=1)


def _store_augmented(project, scale, c_tile, place_ref, ones_ref, out_ref):
    tm = c_tile.shape[0]
    slab = jnp.dot(_split_bf16(c_tile * LOG2E), place_ref[...], preferred_element_type=F32) + ones_ref[...]
    lane = lax.broadcasted_iota(jnp.int32, (tm, LANES), 1)
    own = lane < HEAD_DIM
    bias_lanes = lane < HEAD_DIM + 2 * BIAS_PIECES
    heads_per_piece = MXU_DIM // HEAD_DIM
    for h in range(B_HEADS):
        if h % heads_per_piece == 0:
            x = project(slice(h * HEAD_DIM, h * HEAD_DIM + MXU_DIM)) * scale
        hp = h % heads_per_piece
        pair = x[:, (hp // 2) * LANES:(hp // 2 + 1) * LANES]
        if h % 2 == 1:
            pair = pltpu.roll(pair, HEAD_DIM, 1)
        shift = (HEAD_DIM - BIAS_STRIDE * h) % LANES
        bias = pltpu.roll(slab, shift, 1) if shift else slab
        lane0 = (h % FOX_HEADS) * LANES
        out_ref[h // FOX_HEADS, :, lane0:lane0 + LANES] = jnp.where(
            own, pair, jnp.where(bias_lanes, bias, 0.0)).astype(BF16)


def _kvf_kernel(x_ref, g_ref, wkv_ref, wf_ref, bf_ref, place_ref, ones_ref, gq_ref, wq_ref, placeq_ref,
                onesq_ref, k_ref, v_ref, c_ref, q_ref, carry_ref, *, tm):
    @pl.when(pl.program_id(1) == 0)
    def _():
        carry_ref[...] = jnp.zeros_like(carry_ref)

    x = x_ref[...]
    y = x * lax.rsqrt(jnp.mean(x * x, axis=-1, keepdims=True) + EPS)
    xn = (y * g_ref[...]).astype(BF16)
    xq = (y * gq_ref[...]).astype(BF16)
    z = jnp.dot(xn, wf_ref[...], preferred_element_type=F32) + bf_ref[...]
    log_f = -(jnp.maximum(-z, 0.0) + jnp.log1p(jnp.exp(-jnp.abs(z))))
    rows = lax.broadcasted_iota(jnp.int32, (tm, LANES), 0)
    acc = log_f
    shift = 1
    while shift < tm:
        acc = acc + jnp.where(rows >= shift, pltpu.roll(acc, shift, 0), 0.0)
        shift *= 2
    acc = acc + carry_ref[...]
    c_ref[...] = acc
    carry_ref[...] = acc[tm - 1:tm, :]
    _store_augmented(lambda cols: jnp.dot(xn, wkv_ref[:, cols], preferred_element_type=F32), 1.0,
                     acc, place_ref, ones_ref, k_ref)
    pad_rows = lax.broadcasted_iota(jnp.int32, (VT_ROWS - HEAD_DIM, tm), 0)
    ones_pad = jnp.where(pad_rows == 0, 1.0, 0.0)
    heads_per_piece = MXU_DIM // HEAD_DIM
    for h0 in range(0, B_HEADS, heads_per_piece):
        cols = slice(B_WIDTH + h0 * HEAD_DIM, B_WIDTH + h0 * HEAD_DIM + MXU_DIM)
        vt = jnp.dot(xn, wkv_ref[:, cols], preferred_element_type=F32).T
        for hp in range(heads_per_piece):
            h = h0 + hp
            v_ref[h * VT_ROWS:(h + 1) * VT_ROWS, :] = jnp.concatenate(
                [vt[hp * HEAD_DIM:(hp + 1) * HEAD_DIM], ones_pad], axis=0).astype(BF16)
    _store_augmented(lambda cols: jnp.dot(xq, wq_ref[:, cols], preferred_element_type=F32), SCALE * LOG2E,
                     acc, placeq_ref, onesq_ref, q_ref)


def _kvf_proj(h3, g, wkv, wf, bf, place_k, ones_k, gq, wq_all, q_idx, place_q, ones_q):
    batch, seq, _ = h3.shape
    tm = ROW_TILE
    row = lambda width: pl.BlockSpec((None, tm, width), lambda b, t: (b, t, 0))
    aug = pl.BlockSpec((None, B_HEADS // FOX_HEADS, tm, FOX_HEADS * LANES), lambda b, t: (b, 0, t, 0))
    aug_shape = jax.ShapeDtypeStruct((batch, B_HEADS // FOX_HEADS, seq, FOX_HEADS * LANES), BF16)
    return pl.pallas_call(
        functools.partial(_kvf_kernel, tm=tm),
        grid=(batch, seq // tm),
        in_specs=[row(D_MODEL),
                  _resident((1, D_MODEL), lambda b, t: (0, 0)),
                  _resident((D_MODEL, 2 * B_WIDTH), lambda b, t: (0, 0)),
                  _resident((D_MODEL, LANES), lambda b, t: (0, 0)),
                  _resident((1, LANES), lambda b, t: (0, 0)),
                  _resident((BIAS_PIECES * LANES, LANES), lambda b, t: (0, 0)),
                  _resident((1, LANES), lambda b, t: (0, 0)),
                  _resident((1, D_MODEL), lambda b, t: (0, 0)),
                  _resident((None, D_MODEL, B_WIDTH), lambda b, t: (q_idx, 0, 0)),
                  _resident((BIAS_PIECES * LANES, LANES), lambda b, t: (0, 0)),
                  _resident((1, LANES), lambda b, t: (0, 0))],
        out_specs=[aug,
                   pl.BlockSpec((None, None, B_HEADS * VT_ROWS, tm), lambda b, t: (b, t, 0, 0)),
                   row(LANES), aug],
        out_shape=[aug_shape,
                   jax.ShapeDtypeStruct((batch, seq // tm, B_HEADS * VT_ROWS, tm), BF16),
                   jax.ShapeDtypeStruct((batch, seq, LANES), F32), aug_shape],
        scratch_shapes=[pltpu.VMEM((1, LANES), F32)],
        compiler_params=_params(2),
        name="kvf_proj",
    )(h3, g, wkv, wf, bf, place_k, ones_k, gq, wq_all, place_q, ones_q)


def _qproj_kernel(x_ref, g_ref, w_ref, c_ref, place_ref, ones_ref, o_ref):
    xn = _rms(x_ref[...], g_ref[...]).astype(BF16)
    _store_augmented(lambda cols: jnp.dot(xn, w_ref[:, cols], preferred_element_type=F32), SCALE * LOG2E,
                     c_ref[...], place_ref, ones_ref, o_ref)


def _q_proj(h2, g, w_all, idx, c2, place_q, ones_q, seq):
    m = h2.shape[0]
    tm = ROW_TILE
    nt = seq // tm
    groups = B_HEADS // FOX_HEADS
    return pl.pallas_call(
        _qproj_kernel,
        grid=(m // tm,),
        in_specs=[pl.BlockSpec((tm, D_MODEL), lambda i: (i, 0)),
                  _resident((1, D_MODEL), lambda i: (0, 0)),
                  _resident((None, D_MODEL, B_WIDTH), lambda i: (idx, 0, 0)),
                  pl.BlockSpec((tm, LANES), lambda i: (i, 0)),
                  _resident((BIAS_PIECES * LANES, LANES), lambda i: (0, 0)),
                  _resident((1, LANES), lambda i: (0, 0))],
        out_specs=pl.BlockSpec((None, groups, tm, FOX_HEADS * LANES), lambda i: (i // nt, 0, i % nt, 0)),
        out_shape=jax.ShapeDtypeStruct((m // seq, groups, seq, FOX_HEADS * LANES), BF16),
        compiler_params=_params(1),
        name="q_proj",
    )(h2, g, w_all, c2, place_q, ones_q)


def _fox_kernel(q_ref, k_ref, vt_ref, o_ref, *, tq, nh):
    tqq = tq // 2

    def scores_of(qi, j, key0, nkeys, query0, diagonal):
        return [lax.dot_general(k_ref[j * tq + key0:j * tq + key0 + nkeys, hh * LANES:(hh + 1) * LANES],
                                q_ref[qi * tqq + query0:(qi + 1) * tqq, hh * LANES:(hh + 1) * LANES],
                                NT_DIMS, preferred_element_type=F32) for hh in range(nh)]

    def update(step, scores, state):
        _, j, key0, nkeys, query0, diagonal = step
        stats = []
        for hh in range(nh):
            m = state[hh][0][:, query0:]
            s = scores[hh]
            if diagonal:
                visible = (lax.broadcasted_iota(jnp.int32, s.shape, 0)
                           <= lax.broadcasted_iota(jnp.int32, s.shape, 1))
                s = jnp.where(visible, s, NEG)
            m_new = jnp.maximum(m, jnp.max(s, axis=0, keepdims=True))
            stats.append((m_new, jnp.exp2(m - m_new), jnp.exp2(s - m_new).astype(BF16)))
        out = []
        for hh in range(nh):
            m_new, alpha, p = stats[hh]
            vt = vt_ref[j, hh * VT_ROWS:(hh + 1) * VT_ROWS, key0:key0 + nkeys]
            m_old, acc_old = state[hh]
            acc = alpha * acc_old[:, query0:] + jnp.dot(vt, p, preferred_element_type=F32)
            if query0:
                m_new = jnp.concatenate([m_old[:, :query0], m_new], axis=1)
                acc = jnp.concatenate([acc_old[:, :query0], acc], axis=1)
            out.append((m_new, acc))
        return tuple(out)

    steps = []
    for qi in range(q_ref.shape[0] // tqq):
        jd, off = divmod(qi * tqq, tq)
        steps += [(qi, j, 0, tq, 0, False) for j in range(jd)]
        if off:
            steps.append((qi, jd, 0, off, 0, False))
        steps.append((qi, jd, off, tqq, 0, True))
    scores = scores_of(*steps[0])
    state = None
    for n, step in enumerate(steps):
        nxt = scores_of(*steps[n + 1]) if n + 1 < len(steps) else None
        qi = step[0]
        if n == 0 or steps[n - 1][0] != qi:
            state = tuple((jnp.full((1, tqq), NEG, F32), jnp.zeros((VT_ROWS, tqq), F32)) for _ in range(nh))
        state = update(step, scores, state)
        if step[5]:
            out_t = jnp.concatenate([acc[:HEAD_DIM] / acc[HEAD_DIM:HEAD_DIM + 1] for _, acc in state], axis=0)
            o_ref[qi * tqq:(qi + 1) * tqq, :] = out_t.T.astype(BF16)
        scores = nxt


def _fox_attention(q_aug, k_aug, v_t):
    batch, nt, _, tq = v_t.shape
    assert tq == FOX_TILE
    seq = nt * tq
    nh = FOX_HEADS
    slab = pl.BlockSpec((None, None, seq, nh * LANES), lambda b, hp: (b, hp, 0, 0))
    return pl.pallas_call(
        functools.partial(_fox_kernel, tq=tq, nh=nh),
        grid=(batch, B_HEADS // nh),
        in_specs=[slab, slab,
                  pl.BlockSpec((None, nt, nh * VT_ROWS, tq), lambda b, hp: (b, 0, hp, 0))],
        out_specs=pl.BlockSpec((None, None, seq, nh * HEAD_DIM), lambda b, hp: (b, hp, 0, 0)),
        out_shape=jax.ShapeDtypeStruct((batch, B_HEADS // nh, seq, nh * HEAD_DIM), BF16),
        compiler_params=_params(2),
        name="fox_attn",
    )(q_aug, k_aug, v_t)


def _bias_placement():
    assert 2 * BIAS_PIECES <= BIAS_STRIDE
    place_q = np.zeros((BIAS_PIECES * LANES, LANES), np.float32)
    place_k = np.zeros((BIAS_PIECES * LANES, LANES), np.float32)
    ones_q = np.zeros((1, LANES), np.float32)
    ones_k = np.zeros((1, LANES), np.float32)
    for h in range(B_HEADS):
        base = h * BIAS_STRIDE
        for piece in range(BIAS_PIECES):
            place_q[piece * LANES + h, base + piece] = 1.0
            ones_k[0, base + piece] = 1.0
            place_k[piece * LANES + h, base + BIAS_PIECES + piece] = -1.0
            ones_q[0, base + BIAS_PIECES + piece] = 1.0
    return (jnp.asarray(place_q, BF16), jnp.asarray(ones_q), jnp.asarray(place_k, BF16), jnp.asarray(ones_k))


def _rope_lane_tables(seq):
    pos = jnp.arange(seq, dtype=F32)
    inv = ROPE_THETA ** (-jnp.arange(0, ROPE_DIM, 2, dtype=F32) / ROPE_DIM)
    ang = pos[:, None] * inv[None, :]
    cos, sin = jnp.cos(ang), jnp.sin(ang)
    ones = jnp.ones((seq, HEAD_DIM - ROPE_DIM), F32)
    zeros = jnp.zeros((seq, HEAD_DIM - ROPE_DIM), F32)
    zh = jnp.zeros((seq, ROPE_HALF), F32)
    cos_h = jnp.concatenate([cos, cos, ones], axis=1)
    sa_h = jnp.concatenate([zh, sin, zeros], axis=1)
    sb_h = jnp.concatenate([-sin, zh, zeros], axis=1)
    rep = LANES // HEAD_DIM
    return tuple(jnp.tile(t, (1, rep)) for t in (cos_h, sa_h, sb_h))


def kernel(x, norm_gains, w_qkv_a, w_o_a, w_q_b, w_o_b, kv_norm, w_kvf, b_f, w_up, conv_w, conv_b, w_down):
    batch, seq, _ = x.shape
    m = batch * seq
    cos_t, sa_t, sb_t = _rope_lane_tables(seq)
    place_q, ones_q, place_k, ones_k = _bias_placement()
    gains = norm_gains.reshape(DEPTH * 4, 1, D_MODEL)
    w_qkv_a, w_o_a, w_q_b, w_o_b, w_up, w_down = (
        t.astype(BF16) for t in (w_qkv_a, w_o_a, w_q_b, w_o_b, w_up, w_down))
    w_kv = w_kvf[:, :2 * B_WIDTH].astype(BF16)
    w_f = jnp.pad(w_kvf[:, 2 * B_WIDTH:], ((0, 0), (0, LANES - B_HEADS))).astype(BF16)
    b_f = jnp.pad(b_f, (0, LANES - B_HEADS)).reshape(1, LANES)
    conv_b = conv_b.reshape(DEPTH, 1, 2 * D_FF)

    h = x
    k_aug = v_sh = c_sh = None
    for l in range(DEPTH):
        g = lambda i: gains[4 * l + i]
        if l < N_A:
            qkv = _qkv_proj(h, g(0), w_qkv_a, l, cos_t, sa_t, sb_t)
            mix_inputs, w_o, o_idx = _dilated_attention(qkv), w_o_a, l
        else:
            j = l - N_A
            if l == N_A:
                k_aug, v_sh, c_sh, q_aug = _kvf_proj(h, kv_norm.reshape(1, D_MODEL), w_kv, w_f, b_f,
                                                     place_k, ones_k, g(0), w_q_b, j, place_q, ones_q)
            else:
                q_aug = _q_proj(h.reshape(m, D_MODEL), g(0), w_q_b, j, c_sh.reshape(m, LANES),
                                place_q, ones_q, seq)
            mix_inputs, w_o, o_idx = [_fox_attention(q_aug, k_aug, v_sh)], w_o_b, j
        h = _layer_tail(mix_inputs, w_o, o_idx, h, l, gains, w_up, conv_w, conv_b, w_down)
    return h
```
